```python
import math
import jax, jax.numpy as jnp
from jax import lax
import numpy as np

D_MODEL = 1024
BATCH = 4
SEQ = 8192
DEPTH = 1

D_FF = 2816
N_POOL_GROUPS = 4
POOL_WINDOWS = (2, 4, 8, 16)
POOL_GROUP_DIM = 128
D_POOL = N_POOL_GROUPS * POOL_GROUP_DIM
N_DIFF_HEADS = 8
HEAD_DIM = 64
V_DIM = 2 * HEAD_DIM
D_ATTN = N_DIFF_HEADS * V_DIM
Q_BLOCK = 128
LN_EPS = 1e-5
RMS_EPS = 1e-5
DEEPNORM_ALPHA = (2.0 * DEPTH) ** 0.25
DEEPNORM_BETA = (8.0 * DEPTH) ** (-0.25)
IN_SPLITS = (D_POOL, 2 * N_DIFF_HEADS * HEAD_DIM, 2 * N_DIFF_HEADS * HEAD_DIM, D_ATTN, D_MODEL, D_MODEL)
D_IN = sum(IN_SPLITS)

kernel_name = "hybrid_pool_diffattn_macaron_deepnorm"


def layer_norm(x, g, b):
    xf = x.astype(jnp.float32)
    mu = jnp.mean(xf, axis=-1, keepdims=True)
    var = jnp.mean(jnp.square(xf - mu), axis=-1, keepdims=True)
    y = (xf - mu) * lax.rsqrt(var + LN_EPS)
    return (y * g.astype(jnp.float32) + b.astype(jnp.float32)).astype(x.dtype)


def swiglu(x, w_gate, w_up, w_down):
    return (jax.nn.silu(x @ w_gate) * (x @ w_up)) @ w_down


def alibi_slopes(n_heads):
    return jnp.exp2(-8.0 / n_heads * jnp.arange(1, n_heads + 1, dtype=jnp.float32))


def pool_mixer(u, w_grp, b_grp, scale):
    bsz, s = u.shape[0], u.shape[1]
    ug = u.reshape(bsz, s, N_POOL_GROUPS, POOL_GROUP_DIM).astype(jnp.float32)
    c = jnp.pad(jnp.cumsum(ug, axis=1), ((0, 0), (1, 0), (0, 0), (0, 0)))
    t = jnp.arange(s)
    pooled = []
    for g, w in enumerate(POOL_WINDOWS):
        start = jnp.maximum(t + 1 - w, 0)
        win_sum = c[:, 1:, g] - c[:, start, g]
        cnt = (t + 1 - start).astype(jnp.float32)
        pooled.append(win_sum / cnt[None, :, None])
    z = (jnp.stack(pooled, axis=2) - ug).astype(u.dtype)
    z = jnp.einsum('bsgc,gcd->bsgd', z, w_grp) + b_grp
    return z.reshape(bsz, s, D_POOL) * scale


def diff_attention(q, k, v, lam, slopes):
    bsz, s = q.shape[0], q.shape[1]
    nb = s // Q_BLOCK
    q = q * (HEAD_DIM ** -0.5)
    qb = q.reshape(bsz, nb, Q_BLOCK, 2, N_DIFF_HEADS, HEAD_DIM).transpose(1, 0, 3, 4, 2, 5)
    kt = k.transpose(0, 2, 3, 1, 4)
    vt = v.transpose(0, 2, 1, 3)
    key_pos = jnp.arange(s)

    def block(args):
        qblk, i = args
        q_pos = i * Q_BLOCK + jnp.arange(Q_BLOCK)
        dist = (q_pos[:, None] - key_pos[None, :]).astype(jnp.float32)
        scores = jnp.einsum('bmhqd,bmhkd->bmhqk', qblk, kt).astype(jnp.float32)
        scores = scores - slopes[:, None, None] * dist
        scores = jnp.where(dist >= 0, scores, -jnp.inf)
        p = jax.nn.softmax(scores, axis=-1)
        a = p[:, 0] - lam * p[:, 1]
        return jnp.einsum('bhqk,bhkd->bhqd', a.astype(vt.dtype), vt)

    o = lax.map(block, (qb, jnp.arange(nb)))
    return o.transpose(1, 0, 3, 2, 4).reshape(bsz, s, N_DIFF_HEADS, V_DIM)


def setup_inputs(seed: int = 0) -> dict:
    key = jax.random.key(seed)
    ks = jax.random.split(key, 32)
    f32 = jnp.float32
    L = DEPTH

    def nrm(k, shape, scale):
        return jax.random.normal(k, shape, f32) * scale

    def gain(k, n):
        return jnp.ones((L, n), f32) + nrm(k, (L, n), 0.02)

    return {
        "x": nrm(ks[0], (BATCH, SEQ, D_MODEL), 1.0),
        "ffn1_w_gate": nrm(ks[1], (L, D_MODEL, D_FF), D_MODEL ** -0.5),
        "ffn1_w_up": nrm(ks[2], (L, D_MODEL, D_FF), D_MODEL ** -0.5),
        "ffn1_w_down": nrm(ks[3], (L, D_FF, D_MODEL), D_FF ** -0.5 * DEEPNORM_BETA),
        "ln1_g": gain(ks[4], D_MODEL),
        "ln1_b": nrm(ks[5], (L, D_MODEL), 0.02),
        "w_in": nrm(ks[6], (L, D_MODEL, D_IN), D_MODEL ** -0.5),
        "b_gate": nrm(ks[7], (L, 2 * D_MODEL), 0.02),
        "pool_w": nrm(ks[8], (L, N_POOL_GROUPS, POOL_GROUP_DIM, POOL_GROUP_DIM), POOL_GROUP_DIM ** -0.5),
        "pool_b": nrm(ks[9], (L, N_POOL_GROUPS, POOL_GROUP_DIM), 0.02),
        "pool_scale": gain(ks[10], D_POOL),
        "lambda_q1": nrm(ks[11], (L, HEAD_DIM), 0.1),
        "lambda_k1": nrm(ks[12], (L, HEAD_DIM), 0.1),
        "lambda_q2": nrm(ks[13], (L, HEAD_DIM), 0.1),
        "lambda_k2": nrm(ks[14], (L, HEAD_DIM), 0.1),
        "subln_g": gain(ks[15], V_DIM),
        "w_proj_pool": nrm(ks[16], (L, D_POOL, D_MODEL), D_POOL ** -0.5),
        "w_proj_attn": nrm(ks[17], (L, D_ATTN, D_MODEL), D_ATTN ** -0.5),
        "w_out": nrm(ks[18], (L, D_MODEL, D_MODEL), D_MODEL ** -0.5 * DEEPNORM_BETA),
        "ln2_g": gain(ks[19], D_MODEL),
        "ln2_b": nrm(ks[20], (L, D_MODEL), 0.02),
        "ffn2_w_gate": nrm(ks[21], (L, D_MODEL, D_FF), D_MODEL ** -0.5),
        "ffn2_w_up": nrm(ks[22], (L, D_MODEL, D_FF), D_MODEL ** -0.5),
        "ffn2_w_down": nrm(ks[23], (L, D_FF, D_MODEL), D_FF ** -0.5 * DEEPNORM_BETA),
        "ln3_g": gain(ks[24], D_MODEL),
        "ln3_b": nrm(ks[25], (L, D_MODEL), 0.02),
    }


def reference(x, ffn1_w_gate, ffn1_w_up, ffn1_w_down, ln1_g, ln1_b, w_in, b_gate,
              pool_w, pool_b, pool_scale, lambda_q1, lambda_k1, lambda_q2, lambda_k2,
              subln_g, w_proj_pool, w_proj_attn, w_out, ln2_g, ln2_b,
              ffn2_w_gate, ffn2_w_up, ffn2_w_down, ln3_g, ln3_b):
    bsz, s = x.shape[0], x.shape[1]
    slopes = alibi_slopes(N_DIFF_HEADS)
    split_idx = list(np.cumsum(IN_SPLITS)[:-1])
    for l in range(DEPTH):
        x = layer_norm(DEEPNORM_ALPHA * x + 0.5 * swiglu(x, ffn1_w_gate[l], ffn1_w_up[l], ffn1_w_down[l]),
                       ln1_g[l], ln1_b[l])

        h = x @ w_in[l]
        u_pool, q, k, v, g_a, g_b = jnp.split(h, split_idx, axis=-1)
        gates = jax.nn.sigmoid(jnp.concatenate([g_a, g_b], axis=-1) + b_gate[l])
        g_a, g_b = gates[..., :D_MODEL], gates[..., D_MODEL:]

        y_pool = pool_mixer(u_pool, pool_w[l], pool_b[l], pool_scale[l]) @ w_proj_pool[l]

        lam_init = 0.8 - 0.6 * math.exp(-0.3 * l)
        lam = (jnp.exp(jnp.sum(lambda_q1[l].astype(jnp.float32) * lambda_k1[l].astype(jnp.float32)))
               - jnp.exp(jnp.sum(lambda_q2[l].astype(jnp.float32) * lambda_k2[l].astype(jnp.float32)))
               + lam_init)
        q = q.reshape(bsz, s, 2, N_DIFF_HEADS, HEAD_DIM)
        k = k.reshape(bsz, s, 2, N_DIFF_HEADS, HEAD_DIM)
        v = v.reshape(bsz, s, N_DIFF_HEADS, V_DIM)
        o = diff_attention(q, k, v, lam, slopes)
        of = o.astype(jnp.float32)
        of = of * lax.rsqrt(jnp.mean(jnp.square(of), axis=-1, keepdims=True) + RMS_EPS)
        o = (of * subln_g[l].astype(jnp.float32) * (1.0 - lam_init)).astype(x.dtype)
        y_attn = o.reshape(bsz, s, D_ATTN) @ w_proj_attn[l]

        mix = (g_a * y_pool + g_b * y_attn) @ w_out[l]
        x = layer_norm(DEEPNORM_ALPHA * x + mix, ln2_g[l], ln2_b[l])

        x = layer_norm(DEEPNORM_ALPHA * x + 0.5 * swiglu(x, ffn2_w_gate[l], ffn2_w_up[l], ffn2_w_down[l]),
                       ln3_g[l], ln3_b[l])
    return x
```

```python
import functools
import math

import jax
import jax.numpy as jnp
from jax import lax
from jax.experimental import pallas as pl
from jax.experimental.pallas import tpu as pltpu

F32 = jnp.float32
BF16 = jnp.bfloat16

D_MODEL = 1024
D_FF = 2816
POOL_WINDOWS = (2, 4, 8, 16)
POOL_GROUP_DIM = 128
D_POOL = len(POOL_WINDOWS) * POOL_GROUP_DIM
N_HEADS = 8
HEAD_DIM = 64
V_DIM = 2 * HEAD_DIM
D_ATTN = N_HEADS * V_DIM
LN_EPS = 1e-5
RMS_EPS = 1e-5
DEPTH = 1
DEEPNORM_ALPHA = (2.0 * DEPTH) ** 0.25
LOG2E = math.log2(math.e)
POOL_HISTORY = 16

VMEM_LIMIT_BYTES = 56 * 1024 * 1024

FFN_ROWS = 512
FFN_CHUNK = 256
ATTN_Q = 256
ATTN_K = 256


def _resident(shape):
    nd = len(shape)
    return pl.BlockSpec(shape, lambda *_: (0,) * nd, pipeline_mode=pl.Buffered(1))


def _layer_norm(r, g, b):
    mu = jnp.mean(r, axis=-1, keepdims=True)
    c = r - mu
    var = jnp.mean(c * c, axis=-1, keepdims=True)
    return c * lax.rsqrt(var + LN_EPS) * g + b


def _ffn_ln_kernel(x_ref, wg_ref, wu_ref, wd_ref, g_ref, b_ref, o_ref, h_ref):
    x = x_ref[...]
    xb = x.astype(BF16)
    for c in range(0, D_FF, FFN_CHUNK):
        gate = jnp.dot(xb, wg_ref[:, c:c + FFN_CHUNK], preferred_element_type=F32)
        up = jnp.dot(xb, wu_ref[:, c:c + FFN_CHUNK], preferred_element_type=F32)
        h_ref[:, c:c + FFN_CHUNK] = (gate * jax.nn.sigmoid(gate) * up).astype(BF16)
    y = jnp.dot(h_ref[...], wd_ref[...], preferred_element_type=F32)
    o_ref[...] = _layer_norm(DEEPNORM_ALPHA * x + 0.5 * y, g_ref[...], b_ref[...])


def _ffn_ln(x2d, wg, wu, wd, g, b):
    n = x2d.shape[0]
    tm = FFN_ROWS
    return pl.pallas_call(
        _ffn_ln_kernel,
        grid=(n // tm,),
        in_specs=[
            pl.BlockSpec((tm, D_MODEL), lambda i: (i, 0)),
            _resident((D_MODEL, D_FF)),
            _resident((D_MODEL, D_FF)),
            _resident((D_FF, D_MODEL)),
            _resident((1, D_MODEL)),
            _resident((1, D_MODEL)),
        ],
        out_specs=pl.BlockSpec((tm, D_MODEL), lambda i: (i, 0)),
        out_shape=jax.ShapeDtypeStruct((n, D_MODEL), F32),
        scratch_shapes=[pltpu.VMEM((tm, D_FF), BF16)],
        compiler_params=pltpu.CompilerParams(
            dimension_semantics=("arbitrary",), vmem_limit_bytes=VMEM_LIMIT_BYTES),
        name="ffn_ln",
    )(x2d, wg, wu, wd, g, b)


def _in_proj_kernel(tiles_per_seq, x_ref, wpool_ref, wq_ref, wkt_ref, wv_ref, wg_ref, bg_ref,
                    poolw_ref, poolb_ref, pscale_ref, wpp_ref,
                    q_ref, kt_ref, v_ref, gp_ref, gb_ref, ubuf_ref):
    i = pl.program_id(0)
    tm = x_ref.shape[0]
    xb = x_ref[...].astype(BF16)

    q = jnp.dot(xb, wq_ref[...], preferred_element_type=F32)
    q_ref[...] = (q * (HEAD_DIM ** -0.5 * LOG2E)).astype(BF16)
    kt = lax.dot_general(wkt_ref[...], xb, (((1,), (1,)), ((), ())), preferred_element_type=F32)
    kt_ref[...] = kt.astype(BF16)
    v_ref[...] = jnp.dot(xb, wv_ref[...], preferred_element_type=F32).astype(BF16)

    gates = jax.nn.sigmoid(jnp.dot(xb, wg_ref[...], preferred_element_type=F32) + bg_ref[...])
    gb_ref[...] = gates[:, D_MODEL:]

    hist = POOL_HISTORY
    u = jnp.dot(xb, wpool_ref[...], preferred_element_type=F32)
    seq_tile = i % tiles_per_seq

    @pl.when(seq_tile == 0)
    def _():
        ubuf_ref[0:hist, :] = jnp.zeros((hist, D_POOL), F32)

    @pl.when(seq_tile != 0)
    def _():
        ubuf_ref[0:hist, :] = ubuf_ref[tm:tm + hist, :]

    ubuf_ref[hist:hist + tm, :] = u

    pos = seq_tile * tm + lax.broadcasted_iota(jnp.int32, (tm, 1), 0)
    zs = []
    for g, w in enumerate(POOL_WINDOWS):
        lanes = slice(g * POOL_GROUP_DIM, (g + 1) * POOL_GROUP_DIM)
        ug = u[:, lanes]
        win = ug
        for d in range(1, w):
            win = win + ubuf_ref[hist - d:hist - d + tm, lanes]
        cnt = jnp.minimum(pos + 1, w).astype(F32)
        z = (win / cnt - ug).astype(BF16)
        zg = jnp.dot(z, poolw_ref[g], preferred_element_type=F32) + poolb_ref[g]
        zs.append(zg)
    zall = (jnp.concatenate(zs, axis=-1) * pscale_ref[...]).astype(BF16)
    y_pool = jnp.dot(zall, wpp_ref[...], preferred_element_type=F32)
    gp_ref[...] = gates[:, :D_MODEL] * y_pool


def _in_proj(x2d, seq, wpool, wq, wkt, wv, wg, bg, poolw, poolb, pscale, wpp):
    n = x2d.shape[0]
    bsz = n // seq
    tm = FFN_ROWS
    tps = seq // tm
    row_spec = pl.BlockSpec((tm, D_MODEL), lambda i: (i, 0))
    return pl.pallas_call(
        functools.partial(_in_proj_kernel, tps),
        grid=(n // tm,),
        in_specs=[
            row_spec,
            _resident((D_MODEL, D_POOL)),
            _resident((D_MODEL, D_MODEL)),
            _resident((D_MODEL, D_MODEL)),
            _resident((D_MODEL, D_ATTN)),
            _resident((D_MODEL, 2 * D_MODEL)),
            _resident((1, 2 * D_MODEL)),
            _resident((len(POOL_WINDOWS), POOL_GROUP_DIM, POOL_GROUP_DIM)),
            _resident((len(POOL_WINDOWS), 1, POOL_GROUP_DIM)),
            _resident((1, D_POOL)),
            _resident((D_POOL, D_MODEL)),
        ],
        out_specs=[
            row_spec,
            pl.BlockSpec((None, D_MODEL, tm), lambda i: (i // tps, 0, i % tps)),
            row_spec,
            row_spec,
            row_spec,
        ],
        out_shape=[
            jax.ShapeDtypeStruct((n, D_MODEL), BF16),
            jax.ShapeDtypeStruct((bsz, D_MODEL, seq), BF16),
            jax.ShapeDtypeStruct((n, D_ATTN), BF16),
            jax.ShapeDtypeStruct((n, D_MODEL), F32),
            jax.ShapeDtypeStruct((n, D_MODEL), F32),
        ],
        scratch_shapes=[pltpu.VMEM((POOL_HISTORY + tm, D_POOL), F32)],
        compiler_params=pltpu.CompilerParams(
            dimension_semantics=("arbitrary",), vmem_limit_bytes=VMEM_LIMIT_BYTES),
        name="in_proj",
    )(x2d, wpool, wq, wkt, wv, wg, bg, poolw, poolb, pscale, wpp)


def _attn_kernel(lam_init, slopes_ref, q_ref, kt_ref, v_ref, lamp_ref, subg_ref, o_ref,
                 m_ref, l_ref, acc_ref):
    h = pl.program_id(1)
    i = pl.program_id(2)
    tq = q_ref.shape[0]
    tk = ATTN_K
    slope2 = slopes_ref[h] * LOG2E

    q = q_ref[...]
    lane = lax.broadcasted_iota(jnp.int32, q.shape, 1)
    zero = jnp.zeros_like(q)
    qs = (jnp.where(lane < HEAD_DIM, q, zero), jnp.where(lane >= HEAD_DIM, q, zero))

    m_ref[...] = jnp.full(m_ref.shape, -jnp.inf, F32)
    l_ref[...] = jnp.zeros(l_ref.shape, F32)
    acc_ref[...] = jnp.zeros(acc_ref.shape, F32)

    col = lax.broadcasted_iota(jnp.int32, (1, tk), 1)

    def step(kb, masked):
        start = pl.multiple_of(kb * tk, tk)
        kt = kt_ref[:, pl.ds(start, tk)]
        vv = v_ref[pl.ds(start, tk), :]
        bias = slope2 * (col + (kb * tk - i * tq)).astype(F32)
        for mp in range(2):
            s = jnp.dot(qs[mp], kt, preferred_element_type=F32) + bias
            if masked:
                row = lax.broadcasted_iota(jnp.int32, (tq, tk), 0)
                s = jnp.where(row >= col, s, -jnp.inf)
            m_old = m_ref[mp]
            m_new = jnp.maximum(m_old, jnp.max(s, axis=-1, keepdims=True))
            a = jnp.exp2(m_old - m_new)
            p = jnp.exp2(s - m_new)
            l_ref[mp] = a * l_ref[mp] + jnp.sum(p, axis=-1, keepdims=True)
            acc_ref[mp] = a * acc_ref[mp] + jnp.dot(p.astype(BF16), vv, preferred_element_type=F32)
            m_ref[mp] = m_new

    def body(kb, carry):
        step(kb, False)
        return carry

    lax.fori_loop(0, i, body, 0)
    step(i, True)

    lp = lamp_ref[...]
    lam = (jnp.exp(jnp.sum(lp[0:1] * lp[1:2], axis=-1, keepdims=True))
           - jnp.exp(jnp.sum(lp[2:3] * lp[3:4], axis=-1, keepdims=True)) + lam_init)
    o = acc_ref[0] / l_ref[0] - lam * (acc_ref[1] / l_ref[1])
    o = o * lax.rsqrt(jnp.mean(o * o, axis=-1, keepdims=True) + RMS_EPS)
    o_ref[...] = (o * subg_ref[...] * (1.0 - lam_init)).astype(o_ref.dtype)


def _attention(q, kt, v, slopes, lamp, subg, seq, lam_init):
    n = q.shape[0]
    bsz = n // seq
    tq = ATTN_Q
    nq = seq // tq
    return pl.pallas_call(
        functools.partial(_attn_kernel, lam_init),
        grid=(bsz, N_HEADS, nq),
        in_specs=[
            pl.BlockSpec(memory_space=pltpu.SMEM),
            pl.BlockSpec((tq, V_DIM), lambda b, h, i: (b * nq + i, h)),
            pl.BlockSpec((None, V_DIM, seq), lambda b, h, i: (b, h, 0)),
            pl.BlockSpec((seq, V_DIM), lambda b, h, i: (b, h)),
            pl.BlockSpec((4, HEAD_DIM), lambda b, h, i: (0, 0)),
            pl.BlockSpec((1, V_DIM), lambda b, h, i: (0, 0)),
        ],
        out_specs=pl.BlockSpec((tq, V_DIM), lambda b, h, i: (b * nq + i, h)),
        out_shape=jax.ShapeDtypeStruct((n, D_ATTN), BF16),
        scratch_shapes=[
            pltpu.VMEM((2, tq, 1), F32),
            pltpu.VMEM((2, tq, 1), F32),
            pltpu.VMEM((2, tq, V_DIM), F32),
        ],
        compiler_params=pltpu.CompilerParams(
            dimension_semantics=("arbitrary", "arbitrary", "arbitrary"),
            vmem_limit_bytes=VMEM_LIMIT_BYTES),
        name="diff_attn",
    )(slopes, q, kt, v, lamp, subg)


def _merge_kernel(x_ref, o_ref, gp_ref, gb_ref, wpa_ref, wo_ref, g_ref, b_ref, out_ref):
    y_attn = jnp.dot(o_ref[...], wpa_ref[...], preferred_element_type=F32)
    mixed = (gp_ref[...] + gb_ref[...] * y_attn).astype(BF16)
    mix = jnp.dot(mixed, wo_ref[...], preferred_element_type=F32)
    out_ref[...] = _layer_norm(DEEPNORM_ALPHA * x_ref[...] + mix, g_ref[...], b_ref[...])


def _merge(x2d, o, gp, gb, wpa, wo, g, b):
    n = x2d.shape[0]
    tm = FFN_ROWS
    row_spec = pl.BlockSpec((tm, D_MODEL), lambda i: (i, 0))
    return pl.pallas_call(
        _merge_kernel,
        grid=(n // tm,),
        in_specs=[row_spec, row_spec, row_spec, row_spec,
                  _resident((D_ATTN, D_MODEL)), _resident((D_MODEL, D_MODEL)),
                  _resident((1, D_MODEL)), _resident((1, D_MODEL))],
        out_specs=row_spec,
        out_shape=jax.ShapeDtypeStruct((n, D_MODEL), F32),
        compiler_params=pltpu.CompilerParams(
            dimension_semantics=("arbitrary",), vmem_limit_bytes=VMEM_LIMIT_BYTES),
        name="merge",
    )(x2d, o, gp, gb, wpa, wo, g, b)


def _head_major(w):
    d = w.shape[0]
    return w.reshape(d, 2, N_HEADS, HEAD_DIM).transpose(0, 2, 1, 3).reshape(d, 2 * N_HEADS * HEAD_DIM)


def kernel(x, ffn1_w_gate, ffn1_w_up, ffn1_w_down, ln1_g, ln1_b, w_in, b_gate, pool_w, pool_b, pool_scale, lambda_q1, lambda_k1, lambda_q2, lambda_k2, subln_g, w_proj_pool, w_proj_attn, w_out, ln2_g, ln2_b, ffn2_w_gate, ffn2_w_up, ffn2_w_down, ln3_g, ln3_b):
    bsz, seq, d = x.shape
    n = bsz * seq
    h = x.reshape(n, d)
    slopes = jnp.exp2(-8.0 / N_HEADS * jnp.arange(1, N_HEADS + 1, dtype=F32))
    n_qk = 2 * N_HEADS * HEAD_DIM
    o_q, o_k, o_v, o_g = D_POOL, D_POOL + n_qk, D_POOL + 2 * n_qk, D_POOL + 2 * n_qk + D_ATTN
    for l in range(DEPTH):
        lam_init = 0.8 - 0.6 * math.exp(-0.3 * l)
        h = _ffn_ln(h, ffn1_w_gate[l].astype(BF16), ffn1_w_up[l].astype(BF16),
                    ffn1_w_down[l].astype(BF16), ln1_g[l][None], ln1_b[l][None])

        wi = w_in[l]
        q, kt, v, gp, gb = _in_proj(
            h, seq,
            wi[:, :o_q].astype(BF16),
            _head_major(wi[:, o_q:o_k]).astype(BF16),
            _head_major(wi[:, o_k:o_v]).T.astype(BF16),
            wi[:, o_v:o_g].astype(BF16),
            wi[:, o_g:].astype(BF16),
            b_gate[l][None],
            pool_w[l].astype(BF16),
            pool_b[l][:, None, :],
            pool_scale[l][None],
            w_proj_pool[l].astype(BF16))

        lamp = jnp.stack([lambda_q1[l], lambda_k1[l], lambda_q2[l], lambda_k2[l]]).astype(F32)
        o = _attention(q, kt, v, slopes, lamp, subln_g[l][None].astype(F32), seq, lam_init)

        h = _merge(h, o, gp, gb, w_proj_attn[l].astype(BF16), w_out[l].astype(BF16),
                   ln2_g[l][None], ln2_b[l][None])

        h = _ffn_ln(h, ffn2_w_gate[l].astype(BF16), ffn2_w_up[l].astype(BF16),
                    ffn2_w_down[l].astype(BF16), ln3_g[l][None], ln3_b[l][None])
    return h.reshape(bsz, seq, d)
```

```python
import functools
import math

import jax
import jax.numpy as jnp
from jax import lax
from jax.experimental import pallas as pl
from jax.experimental.pallas import tpu as pltpu

F32 = jnp.float32
BF16 = jnp.bfloat16

D_MODEL = 1024
D_FF = 2816
POOL_WINDOWS = (2, 4, 8, 16)
POOL_GROUP_DIM = 128
D_POOL = len(POOL_WINDOWS) * POOL_GROUP_DIM
N_HEADS = 8
HEAD_DIM = 64
V_DIM = 2 * HEAD_DIM
D_ATTN = N_HEADS * V_DIM
D_QK = N_HEADS * HEAD_DIM
LN_EPS = 1e-5
RMS_EPS = 1e-5
DEPTH = 1
DEEPNORM_ALPHA = (2.0 * DEPTH) ** 0.25
LOG2E = math.log2(math.e)
POOL_HISTORY = 16
ONES_ROWS = 16
SLAB = 2 * HEAD_DIM
BF16_EXACT_INT = 256
N_SPLIT = 3

VMEM_LIMIT_BYTES = 56 * 1024 * 1024

FFN_ROWS = 512
FFN_CHUNK = 256
ATTN_Q = 512
ATTN_K = 512


def _resident(shape):
    nd = len(shape)
    return pl.BlockSpec(shape, lambda *_: (0,) * nd, pipeline_mode=pl.Buffered(1))


def _layer_norm(r, g, b):
    mu = jnp.mean(r, axis=-1, keepdims=True)
    c = r - mu
    var = jnp.mean(c * c, axis=-1, keepdims=True)
    return c * lax.rsqrt(var + LN_EPS) * g + b


def _ffn_ln_kernel(x_ref, wg_ref, wu_ref, wd_ref, g_ref, b_ref, o_ref, h_ref):
    x = x_ref[...]
    xb = x.astype(BF16)
    for c in range(0, D_FF, FFN_CHUNK):
        gate = jnp.dot(xb, wg_ref[:, c:c + FFN_CHUNK], preferred_element_type=F32)
        up = jnp.dot(xb, wu_ref[:, c:c + FFN_CHUNK], preferred_element_type=F32)
        h_ref[:, c:c + FFN_CHUNK] = (gate * jax.nn.sigmoid(gate) * up).astype(BF16)
    y = jnp.dot(h_ref[...], wd_ref[...], preferred_element_type=F32)
    o_ref[...] = _layer_norm(DEEPNORM_ALPHA * x + 0.5 * y, g_ref[...], b_ref[...])


def _ffn_ln(x2d, wg, wu, wd, g, b):
    n = x2d.shape[0]
    tm = FFN_ROWS
    return pl.pallas_call(
        _ffn_ln_kernel,
        grid=(n // tm,),
        in_specs=[
            pl.BlockSpec((tm, D_MODEL), lambda i: (i, 0)),
            _resident((D_MODEL, D_FF)),
            _resident((D_MODEL, D_FF)),
            _resident((D_FF, D_MODEL)),
            _resident((1, D_MODEL)),
            _resident((1, D_MODEL)),
        ],
        out_specs=pl.BlockSpec((tm, D_MODEL), lambda i: (i, 0)),
        out_shape=jax.ShapeDtypeStruct((n, D_MODEL), F32),
        scratch_shapes=[pltpu.VMEM((tm, D_FF), BF16)],
        compiler_params=pltpu.CompilerParams(
            dimension_semantics=("arbitrary",), vmem_limit_bytes=VMEM_LIMIT_BYTES),
        name="ffn_ln",
    )(x2d, wg, wu, wd, g, b)


def _in_proj_kernel(tiles_per_seq, x_ref, wpool_ref, wqt_ref, wk_ref, wvt_ref, wg_ref, bg_ref,
                    qaux_ref, poolw_ref, poolb_ref, pscale_ref, wpp_ref,
                    qt_ref, k_ref, vt_ref, gp_ref, gb_ref, ubuf_ref):
    i = pl.program_id(0)
    tm = x_ref.shape[0]
    xb = x_ref[...].astype(BF16)
    seq_tile = i % tiles_per_seq
    nt = (((1,), (1,)), ((), ()))

    qt = lax.dot_general(wqt_ref[...], xb, nt, preferred_element_type=F32)
    qt = (qt * (HEAD_DIM ** -0.5 * LOG2E)).astype(BF16)
    for hd in range(N_HEADS):
        aux = jnp.broadcast_to(qaux_ref[hd * HEAD_DIM:(hd + 1) * HEAD_DIM, :],
                               (HEAD_DIM, tm)).astype(BF16)
        feat_rows = hd * SLAB + (hd % 2) * HEAD_DIM
        aux_rows = hd * SLAB + (1 - hd % 2) * HEAD_DIM
        for mp in range(2):
            src = mp * D_QK + hd * HEAD_DIM
            qt_ref[mp, feat_rows:feat_rows + HEAD_DIM, :] = qt[src:src + HEAD_DIM]
            qt_ref[mp, aux_rows:aux_rows + HEAD_DIM, :] = aux

    k = jnp.dot(xb, wk_ref[...], preferred_element_type=F32)
    row = lax.broadcasted_iota(jnp.int32, (tm, SLAB), 0)
    lane = lax.broadcasted_iota(jnp.int32, (tm, SLAB), 1)
    j = (seq_tile * tm + row) & (ATTN_K - 1)
    jlo = (j & (BF16_EXACT_INT - 1)).astype(F32)
    jhi = (j - (j & (BF16_EXACT_INT - 1))).astype(F32)
    low = lane < HEAD_DIM
    al = lane & (HEAD_DIM - 1)
    aux = jnp.where(al < N_SPLIT, jlo, jnp.where(al < 2 * N_SPLIT, jhi, 0.0))
    for mp in range(2):
        for pair in range(N_HEADS // 2):
            src = mp * D_QK + pair * SLAB
            feats = k[:, src:src + SLAB]
            dst = 2 * pair * SLAB
            k_ref[mp, :, dst:dst + SLAB] = jnp.where(low, feats, aux).astype(BF16)
            k_ref[mp, :, dst + SLAB:dst + 2 * SLAB] = jnp.where(low, aux, feats).astype(BF16)

    vt = lax.dot_general(wvt_ref[...], xb, nt, preferred_element_type=F32)
    vrows = V_DIM + ONES_ROWS
    for hd in range(N_HEADS):
        vt_ref[hd * vrows:hd * vrows + V_DIM, :] = vt[hd * V_DIM:(hd + 1) * V_DIM].astype(BF16)
        vt_ref[hd * vrows + V_DIM:(hd + 1) * vrows, :] = jnp.ones((ONES_ROWS, tm), BF16)

    gates = jax.nn.sigmoid(jnp.dot(xb, wg_ref[...], preferred_element_type=F32) + bg_ref[...])
    gb_ref[...] = gates[:, D_MODEL:]

    hist = POOL_HISTORY
    u = jnp.dot(xb, wpool_ref[...], preferred_element_type=F32)

    @pl.when(seq_tile == 0)
    def _():
        ubuf_ref[0:hist, :] = jnp.zeros((hist, D_POOL), F32)

    @pl.when(seq_tile != 0)
    def _():
        ubuf_ref[0:hist, :] = ubuf_ref[tm:tm + hist, :]

    ubuf_ref[hist:hist + tm, :] = u

    pos = seq_tile * tm + lax.broadcasted_iota(jnp.int32, (tm, 1), 0)
    zs = []
    for g, w in enumerate(POOL_WINDOWS):
        lanes = slice(g * POOL_GROUP_DIM, (g + 1) * POOL_GROUP_DIM)
        ug = u[:, lanes]
        win = ug
        for d in range(1, w):
            win = win + ubuf_ref[hist - d:hist - d + tm, lanes]
        cnt = jnp.minimum(pos + 1, w).astype(F32)
        z = (win / cnt - ug).astype(BF16)
        zg = jnp.dot(z, poolw_ref[g], preferred_element_type=F32) + poolb_ref[g]
        zs.append(zg)
    zall = (jnp.concatenate(zs, axis=-1) * pscale_ref[...]).astype(BF16)
    y_pool = jnp.dot(zall, wpp_ref[...], preferred_element_type=F32)
    gp_ref[...] = gates[:, :D_MODEL] * y_pool


def _in_proj(x2d, seq, wpool, wqt, wk, wvt, wg, bg, qaux, poolw, poolb, pscale, wpp):
    n = x2d.shape[0]
    bsz = n // seq
    tm = FFN_ROWS
    tps = seq // tm
    row_spec = pl.BlockSpec((tm, D_MODEL), lambda i: (i, 0))
    vt_rows = N_HEADS * (V_DIM + ONES_ROWS)
    return pl.pallas_call(
        functools.partial(_in_proj_kernel, tps),
        grid=(n // tm,),
        in_specs=[
            row_spec,
            _resident((D_MODEL, D_POOL)),
            _resident((2 * D_QK, D_MODEL)),
            _resident((D_MODEL, 2 * D_QK)),
            _resident((D_ATTN, D_MODEL)),
            _resident((D_MODEL, 2 * D_MODEL)),
            _resident((1, 2 * D_MODEL)),
            _resident((D_QK, 1)),
            _resident((len(POOL_WINDOWS), POOL_GROUP_DIM, POOL_GROUP_DIM)),
            _resident((len(POOL_WINDOWS), 1, POOL_GROUP_DIM)),
            _resident((1, D_POOL)),
            _resident((D_POOL, D_MODEL)),
        ],
        out_specs=[
            pl.BlockSpec((None, 2, N_HEADS * SLAB, tm), lambda i: (i // tps, 0, 0, i % tps)),
            pl.BlockSpec((2, tm, N_HEADS * SLAB), lambda i: (0, i, 0)),
            pl.BlockSpec((None, vt_rows, tm), lambda i: (i // tps, 0, i % tps)),
            row_spec,
            row_spec,
        ],
        out_shape=[
            jax.ShapeDtypeStruct((bsz, 2, N_HEADS * SLAB, seq), BF16),
            jax.ShapeDtypeStruct((2, n, N_HEADS * SLAB), BF16),
            jax.ShapeDtypeStruct((bsz, vt_rows, seq), BF16),
            jax.ShapeDtypeStruct((n, D_MODEL), F32),
            jax.ShapeDtypeStruct((n, D_MODEL), F32),
        ],
        scratch_shapes=[pltpu.VMEM((POOL_HISTORY + tm, D_POOL), F32)],
        compiler_params=pltpu.CompilerParams(
            dimension_semantics=("arbitrary",), vmem_limit_bytes=VMEM_LIMIT_BYTES),
        name="in_proj",
    )(x2d, wpool, wqt, wk, wvt, wg, bg, qaux, poolw, poolb, pscale, wpp)


def _attn_kernel(lam_init, slopes_ref, qt_ref, k_ref, vt_ref, lamp_ref, subg_ref, o_ref,
                 acc_ref, s_ref, p_ref):
    h = pl.program_id(1)
    i = pl.program_id(2)
    tq = qt_ref.shape[2]
    tk = ATTN_K
    slope2 = slopes_ref[h] * LOG2E

    acc_ref[...] = jnp.zeros(acc_ref.shape, F32)
    p_ref[...] = jnp.zeros(p_ref.shape, BF16)

    def scores(kb, diagonal):
        start = pl.multiple_of(kb * tk, tk)
        maxes = []
        for mp in range(2):
            s = jnp.dot(k_ref[mp, pl.ds(start, tk), :], qt_ref[mp], preferred_element_type=F32)
            if diagonal:
                key = lax.broadcasted_iota(jnp.int32, (tk, tq), 0)
                qry = lax.broadcasted_iota(jnp.int32, (tk, tq), 1)
                s = jnp.where(key <= qry, s, -jnp.inf)
            s_ref[mp] = s
            maxes.append(jnp.max(s, axis=0, keepdims=True))
        return tuple(maxes)

    def numerators(kb, ms, maxes):
        off = slope2 * (kb * tk - i * tq).astype(F32)
        new_ms, scales = [], []
        for mp in range(2):
            m_new = jnp.maximum(ms[mp], maxes[mp] + off)
            scales.append(jnp.exp2(ms[mp] - m_new))
            p_ref[mp] = jnp.exp2(s_ref[mp] - (m_new - off)).astype(BF16)
            new_ms.append(m_new)
        return tuple(new_ms), tuple(scales)

    def accumulate(kb, scales):
        start = pl.multiple_of(kb * tk, tk)
        vt = vt_ref[:, pl.ds(start, tk)]
        for mp in range(2):
            acc_ref[mp] = scales[mp] * acc_ref[mp] + jnp.dot(
                vt, p_ref[mp], preferred_element_type=F32)

    def block_of(t):
        return jnp.where(t <= 0, i, t - 1)

    def body(t, carry):
        ms, scales, maxes = carry
        accumulate(block_of(t - 1), scales)
        ms, scales = numerators(block_of(t), ms, maxes)
        maxes = scores(t, False)
        return ms, scales, maxes

    m0 = jnp.full((1, tq), -jnp.inf, F32)
    one = jnp.ones((1, tq), F32)
    maxes = scores(i, True)
    ms, scales, maxes = lax.fori_loop(0, i, body, ((m0, m0), (one, one), maxes))
    accumulate(block_of(i - 1), scales)
    ms, scales = numerators(block_of(i), ms, maxes)
    accumulate(block_of(i), scales)

    lp = lamp_ref[...]
    lam = (jnp.exp(jnp.sum(lp[0:1] * lp[1:2], axis=-1, keepdims=True))
           - jnp.exp(jnp.sum(lp[2:3] * lp[3:4], axis=-1, keepdims=True)) + lam_init)
    acc1 = acc_ref[0]
    acc2 = acc_ref[1]
    o = acc1[:V_DIM] / acc1[V_DIM:V_DIM + 1] - lam * (acc2[:V_DIM] / acc2[V_DIM:V_DIM + 1])
    o = o * lax.rsqrt(jnp.mean(o * o, axis=0, keepdims=True) + RMS_EPS)
    o = o * subg_ref[...] * (1.0 - lam_init)
    o_ref[...] = o.T.astype(o_ref.dtype)


def _attention(qt, k, vt, slopes, lamp, subg, lam_init):
    bsz, _, _, seq = qt.shape
    n = bsz * seq
    tq = ATTN_Q
    nq = seq // tq
    vrows = V_DIM + ONES_ROWS
    return pl.pallas_call(
        functools.partial(_attn_kernel, lam_init),
        grid=(bsz, N_HEADS, nq),
        in_specs=[
            pl.BlockSpec(memory_space=pltpu.SMEM),
            pl.BlockSpec((None, 2, SLAB, tq), lambda b, h, i: (b, 0, h, i)),
            pl.BlockSpec((2, seq, SLAB), lambda b, h, i: (0, b, h)),
            pl.BlockSpec((None, vrows, seq), lambda b, h, i: (b, h, 0)),
            pl.BlockSpec((4, HEAD_DIM), lambda b, h, i: (0, 0)),
            pl.BlockSpec((V_DIM, 1), lambda b, h, i: (0, 0)),
        ],
        out_specs=pl.BlockSpec((tq, V_DIM), lambda b, h, i: (b * nq + i, h)),
        out_shape=jax.ShapeDtypeStruct((n, D_ATTN), BF16),
        scratch_shapes=[
            pltpu.VMEM((2, vrows, tq), F32),
            pltpu.VMEM((2, ATTN_K, tq), F32),
            pltpu.VMEM((2, ATTN_K, tq), BF16),
        ],
        compiler_params=pltpu.CompilerParams(
            dimension_semantics=("arbitrary", "arbitrary", "arbitrary"),
            vmem_limit_bytes=VMEM_LIMIT_BYTES),
        name="diff_attn",
    )(slopes, qt, k, vt, lamp, subg)


def _merge_kernel(x_ref, o_ref, gp_ref, gb_ref, wpa_ref, wo_ref, g_ref, b_ref, out_ref):
    y_attn = jnp.dot(o_ref[...], wpa_ref[...], preferred_element_type=F32)
    mixed = (gp_ref[...] + gb_ref[...] * y_attn).astype(BF16)
    mix = jnp.dot(mixed, wo_ref[...], preferred_element_type=F32)
    out_ref[...] = _layer_norm(DEEPNORM_ALPHA * x_ref[...] + mix, g_ref[...], b_ref[...])


def _merge(x2d, o, gp, gb, wpa, wo, g, b):
    n = x2d.shape[0]
    tm = FFN_ROWS
    row_spec = pl.BlockSpec((tm, D_MODEL), lambda i: (i, 0))
    return pl.pallas_call(
        _merge_kernel,
        grid=(n // tm,),
        in_specs=[row_spec, row_spec, row_spec, row_spec,
                  _resident((D_ATTN, D_MODEL)), _resident((D_MODEL, D_MODEL)),
                  _resident((1, D_MODEL)), _resident((1, D_MODEL))],
        out_specs=row_spec,
        out_shape=jax.ShapeDtypeStruct((n, D_MODEL), F32),
        compiler_params=pltpu.CompilerParams(
            dimension_semantics=("arbitrary",), vmem_limit_bytes=VMEM_LIMIT_BYTES),
        name="merge",
    )(x2d, o, gp, gb, wpa, wo, g, b)


def _slope_pieces(slopes):
    c = slopes * LOG2E
    pieces = []
    for _ in range(N_SPLIT):
        piece = c.astype(BF16).astype(F32)
        pieces.append(piece)
        c = c - piece
    aux = jnp.zeros((N_HEADS, HEAD_DIM), F32)
    aux = aux.at[:, :2 * N_SPLIT].set(jnp.stack(pieces + pieces, axis=1))
    return aux.reshape(D_QK, 1)


def kernel(x, ffn1_w_gate, ffn1_w_up, ffn1_w_down, ln1_g, ln1_b, w_in, b_gate, pool_w, pool_b, pool_scale, lambda_q1, lambda_k1, lambda_q2, lambda_k2, subln_g, w_proj_pool, w_proj_attn, w_out, ln2_g, ln2_b, ffn2_w_gate, ffn2_w_up, ffn2_w_down, ln3_g, ln3_b):
    bsz, seq, d = x.shape
    n = bsz * seq
    h = x.reshape(n, d)
    slopes = jnp.exp2(-8.0 / N_HEADS * jnp.arange(1, N_HEADS + 1, dtype=F32))
    qaux = _slope_pieces(slopes)
    o_q, o_k, o_v, o_g = D_POOL, D_POOL + 2 * D_QK, D_POOL + 4 * D_QK, D_POOL + 4 * D_QK + D_ATTN
    for l in range(DEPTH):
        lam_init = 0.8 - 0.6 * math.exp(-0.3 * l)
        h = _ffn_ln(h, ffn1_w_gate[l].astype(BF16), ffn1_w_up[l].astype(BF16),
                    ffn1_w_down[l].astype(BF16), ln1_g[l][None], ln1_b[l][None])

        wi = w_in[l]
        qt, k, vt, gp, gb = _in_proj(
            h, seq,
            wi[:, :o_q].astype(BF16),
            wi[:, o_q:o_k].T.astype(BF16),
            wi[:, o_k:o_v].astype(BF16),
            wi[:, o_v:o_g].T.astype(BF16),
            wi[:, o_g:].astype(BF16),
            b_gate[l][None],
            qaux,
            pool_w[l].astype(BF16),
            pool_b[l][:, None, :],
            pool_scale[l][None],
            w_proj_pool[l].astype(BF16))

        lamp = jnp.stack([lambda_q1[l], lambda_k1[l], lambda_q2[l], lambda_k2[l]]).astype(F32)
        o = _attention(qt, k, vt, slopes, lamp, subln_g[l][:, None].astype(F32), lam_init)

        h = _merge(h, o, gp, gb, w_proj_attn[l].astype(BF16), w_out[l].astype(BF16),
                   ln2_g[l][None], ln2_b[l][None])

        h = _ffn_ln(h, ffn2_w_gate[l].astype(BF16), ffn2_w_up[l].astype(BF16),
                    ffn2_w_down[l].astype(BF16), ln3_g[l][None], ln3_b[l][None])
    return h.reshape(bsz, seq, d)
```

```python
import functools
import math

import jax
import jax.numpy as jnp
from jax import lax
from jax.experimental import pallas as pl
from jax.experimental.pallas import tpu as pltpu

F32 = jnp.float32
BF16 = jnp.bfloat16

D_MODEL = 1024
D_FF = 2816
POOL_WINDOWS = (2, 4, 8, 16)
POOL_GROUP_DIM = 128
D_POOL = len(POOL_WINDOWS) * POOL_GROUP_DIM
N_HEADS = 8
HEAD_DIM = 64
V_DIM = 2 * HEAD_DIM
D_ATTN = N_HEADS * V_DIM
D_QK = N_HEADS * HEAD_DIM
LN_EPS = 1e-5
RMS_EPS = 1e-5
DEPTH = 1
DEEPNORM_ALPHA = (2.0 * DEPTH) ** 0.25
LOG2E = math.log2(math.e)
POOL_HISTORY = 16
ONES_ROWS = 16
SLAB = 2 * HEAD_DIM
BF16_EXACT_INT = 256
N_SPLIT = 3

VMEM_LIMIT_BYTES = 56 * 1024 * 1024

FFN_ROWS = 512
FFN_CHUNK = 256
ATTN_K = 512
ATTN_Q = 1024


def _resident(shape):
    nd = len(shape)
    return pl.BlockSpec(shape, lambda *_: (0,) * nd, pipeline_mode=pl.Buffered(1))


def _layer_norm(r, g, b):
    mu = jnp.mean(r, axis=-1, keepdims=True)
    c = r - mu
    var = jnp.mean(c * c, axis=-1, keepdims=True)
    return c * lax.rsqrt(var + LN_EPS) * g + b


def _ffn_ln_kernel(x_ref, wg_ref, wu_ref, wd_ref, g_ref, b_ref, o_ref, h_ref):
    x = x_ref[...]
    xb = x.astype(BF16)
    for c in range(0, D_FF, FFN_CHUNK):
        gate = jnp.dot(xb, wg_ref[:, c:c + FFN_CHUNK], preferred_element_type=F32)
        up = jnp.dot(xb, wu_ref[:, c:c + FFN_CHUNK], preferred_element_type=F32)
        h_ref[:, c:c + FFN_CHUNK] = (gate * jax.nn.sigmoid(gate) * up).astype(BF16)
    y = jnp.dot(h_ref[...], wd_ref[...], preferred_element_type=F32)
    o_ref[...] = _layer_norm(DEEPNORM_ALPHA * x + 0.5 * y, g_ref[...], b_ref[...])


def _ffn_ln(x2d, wg, wu, wd, g, b):
    n = x2d.shape[0]
    tm = FFN_ROWS
    return pl.pallas_call(
        _ffn_ln_kernel,
        grid=(n // tm,),
        in_specs=[
            pl.BlockSpec((tm, D_MODEL), lambda i: (i, 0)),
            _resident((D_MODEL, D_FF)),
            _resident((D_MODEL, D_FF)),
            _resident((D_FF, D_MODEL)),
            _resident((1, D_MODEL)),
            _resident((1, D_MODEL)),
        ],
        out_specs=pl.BlockSpec((tm, D_MODEL), lambda i: (i, 0)),
        out_shape=jax.ShapeDtypeStruct((n, D_MODEL), F32),
        scratch_shapes=[pltpu.VMEM((tm, D_FF), BF16)],
        compiler_params=pltpu.CompilerParams(
            dimension_semantics=("arbitrary",), vmem_limit_bytes=VMEM_LIMIT_BYTES),
        name="ffn_ln",
    )(x2d, wg, wu, wd, g, b)


def _in_proj_kernel(tiles_per_seq, x_ref, wpool_ref, wqt_ref, wk_ref, wvt_ref, wg_ref, bg_ref,
                    qaux_ref, poolw_ref, poolb_ref, pscale_ref, wpp_ref,
                    qt_ref, k_ref, vt_ref, gp_ref, gb_ref, ubuf_ref):
    i = pl.program_id(0)
    tm = x_ref.shape[0]
    xb = x_ref[...].astype(BF16)
    seq_tile = i % tiles_per_seq
    nt = (((1,), (1,)), ((), ()))

    qt = lax.dot_general(wqt_ref[...], xb, nt, preferred_element_type=F32)
    qt = (qt * (HEAD_DIM ** -0.5 * LOG2E)).astype(BF16)
    for hd in range(N_HEADS):
        aux = jnp.broadcast_to(qaux_ref[hd * HEAD_DIM:(hd + 1) * HEAD_DIM, :],
                               (HEAD_DIM, tm)).astype(BF16)
        feat_rows = hd * SLAB + (hd % 2) * HEAD_DIM
        aux_rows = hd * SLAB + (1 - hd % 2) * HEAD_DIM
        for mp in range(2):
            src = mp * D_QK + hd * HEAD_DIM
            qt_ref[mp, feat_rows:feat_rows + HEAD_DIM, :] = qt[src:src + HEAD_DIM]
            qt_ref[mp, aux_rows:aux_rows + HEAD_DIM, :] = aux

    k = jnp.dot(xb, wk_ref[...], preferred_element_type=F32)
    row = lax.broadcasted_iota(jnp.int32, (tm, SLAB), 0)
    lane = lax.broadcasted_iota(jnp.int32, (tm, SLAB), 1)
    j = (seq_tile * tm + row) & (ATTN_K - 1)
    jlo = (j & (BF16_EXACT_INT - 1)).astype(F32)
    jhi = (j - (j & (BF16_EXACT_INT - 1))).astype(F32)
    low = lane < HEAD_DIM
    al = lane & (HEAD_DIM - 1)
    aux = jnp.where(al < N_SPLIT, jlo, jnp.where(al < 2 * N_SPLIT, jhi, 0.0))
    for mp in range(2):
        for pair in range(N_HEADS // 2):
            src = mp * D_QK + pair * SLAB
            feats = k[:, src:src + SLAB]
            dst = 2 * pair * SLAB
            k_ref[mp, :, dst:dst + SLAB] = jnp.where(low, feats, aux).astype(BF16)
            k_ref[mp, :, dst + SLAB:dst + 2 * SLAB] = jnp.where(low, aux, feats).astype(BF16)

    vt = lax.dot_general(wvt_ref[...], xb, nt, preferred_element_type=F32)
    vrows = V_DIM + ONES_ROWS
    for hd in range(N_HEADS):
        vt_ref[hd * vrows:hd * vrows + V_DIM, :] = vt[hd * V_DIM:(hd + 1) * V_DIM].astype(BF16)
        vt_ref[hd * vrows + V_DIM:(hd + 1) * vrows, :] = jnp.ones((ONES_ROWS, tm), BF16)

    gates = jax.nn.sigmoid(jnp.dot(xb, wg_ref[...], preferred_element_type=F32) + bg_ref[...])
    gb_ref[...] = gates[:, D_MODEL:]

    hist = POOL_HISTORY
    u = jnp.dot(xb, wpool_ref[...], preferred_element_type=F32)

    @pl.when(seq_tile == 0)
    def _():
        ubuf_ref[0:hist, :] = jnp.zeros((hist, D_POOL), F32)

    @pl.when(seq_tile != 0)
    def _():
        ubuf_ref[0:hist, :] = ubuf_ref[tm:tm + hist, :]

    ubuf_ref[hist:hist + tm, :] = u

    pos = seq_tile * tm + lax.broadcasted_iota(jnp.int32, (tm, 1), 0)
    zs = []
    for g, w in enumerate(POOL_WINDOWS):
        lanes = slice(g * POOL_GROUP_DIM, (g + 1) * POOL_GROUP_DIM)
        ug = u[:, lanes]
        win = ug
        for d in range(1, w):
            win = win + ubuf_ref[hist - d:hist - d + tm, lanes]
        cnt = jnp.minimum(pos + 1, w).astype(F32)
        z = (win / cnt - ug).astype(BF16)
        zg = jnp.dot(z, poolw_ref[g], preferred_element_type=F32) + poolb_ref[g]
        zs.append(zg)
    zall = (jnp.concatenate(zs, axis=-1) * pscale_ref[...]).astype(BF16)
    y_pool = jnp.dot(zall, wpp_ref[...], preferred_element_type=F32)
    gp_ref[...] = gates[:, :D_MODEL] * y_pool


def _in_proj(x2d, seq, wpool, wqt, wk, wvt, wg, bg, qaux, poolw, poolb, pscale, wpp):
    n = x2d.shape[0]
    bsz = n // seq
    tm = FFN_ROWS
    tps = seq // tm
    row_spec = pl.BlockSpec((tm, D_MODEL), lambda i: (i, 0))
    vt_rows = N_HEADS * (V_DIM + ONES_ROWS)
    return pl.pallas_call(
        functools.partial(_in_proj_kernel, tps),
        grid=(n // tm,),
        in_specs=[
            row_spec,
            _resident((D_MODEL, D_POOL)),
            _resident((2 * D_QK, D_MODEL)),
            _resident((D_MODEL, 2 * D_QK)),
            _resident((D_ATTN, D_MODEL)),
            _resident((D_MODEL, 2 * D_MODEL)),
            _resident((1, 2 * D_MODEL)),
            _resident((D_QK, 1)),
            _resident((len(POOL_WINDOWS), POOL_GROUP_DIM, POOL_GROUP_DIM)),
            _resident((len(POOL_WINDOWS), 1, POOL_GROUP_DIM)),
            _resident((1, D_POOL)),
            _resident((D_POOL, D_MODEL)),
        ],
        out_specs=[
            pl.BlockSpec((None, 2, N_HEADS * SLAB, tm), lambda i: (i // tps, 0, 0, i % tps)),
            pl.BlockSpec((2, tm, N_HEADS * SLAB), lambda i: (0, i, 0)),
            pl.BlockSpec((None, vt_rows, tm), lambda i: (i // tps, 0, i % tps)),
            row_spec,
            row_spec,
        ],
        out_shape=[
            jax.ShapeDtypeStruct((bsz, 2, N_HEADS * SLAB, seq), BF16),
            jax.ShapeDtypeStruct((2, n, N_HEADS * SLAB), BF16),
            jax.ShapeDtypeStruct((bsz, vt_rows, seq), BF16),
            jax.ShapeDtypeStruct((n, D_MODEL), F32),
            jax.ShapeDtypeStruct((n, D_MODEL), F32),
        ],
        scratch_shapes=[pltpu.VMEM((POOL_HISTORY + tm, D_POOL), F32)],
        compiler_params=pltpu.CompilerParams(
            dimension_semantics=("arbitrary",), vmem_limit_bytes=VMEM_LIMIT_BYTES),
        name="in_proj",
    )(x2d, wpool, wqt, wk, wvt, wg, bg, qaux, poolw, poolb, pscale, wpp)


def _attn_kernel(lam_init, slopes_ref, qt_ref, k_ref, vt_ref, lamp_ref, subg_ref, o_ref,
                 acc_ref, s_ref, p_ref):
    h = pl.program_id(1)
    pair = pl.program_id(2)
    tk = ATTN_K
    slope2 = slopes_ref[h] * LOG2E
    n_chain = ATTN_Q // tk

    acc_ref[...] = jnp.zeros(acc_ref.shape, F32)
    p_ref[...] = jnp.zeros(p_ref.shape, BF16)

    def q_block(c):
        return n_chain * pair + c

    def scores(c, kb, diagonal):
        start = pl.multiple_of(kb * tk, tk)
        maxes = []
        for mp in range(2):
            s = jnp.dot(k_ref[mp, pl.ds(start, tk), :], qt_ref[mp, :, c * tk:(c + 1) * tk],
                        preferred_element_type=F32)
            if diagonal:
                key = lax.broadcasted_iota(jnp.int32, (tk, tk), 0)
                qry = lax.broadcasted_iota(jnp.int32, (tk, tk), 1)
                s = jnp.where(key <= qry, s, -jnp.inf)
            s_ref[c, mp] = s
            maxes.append(jnp.max(s, axis=0, keepdims=True))
        return tuple(maxes)

    def numerators(c, kb, ms, maxes):
        off = slope2 * ((kb - q_block(c)) * tk).astype(F32)
        new_ms, scales = [], []
        for mp in range(2):
            m_new = jnp.maximum(ms[mp], maxes[mp] + off)
            scales.append(jnp.exp2(ms[mp] - m_new))
            p_ref[c, mp] = jnp.exp2(s_ref[c, mp] - (m_new - off)).astype(BF16)
            new_ms.append(m_new)
        return tuple(new_ms), tuple(scales)

    def accumulate(c, kb, scales):
        start = pl.multiple_of(kb * tk, tk)
        vt = vt_ref[:, pl.ds(start, tk)]
        for mp in range(2):
            acc_ref[c, mp] = scales[mp] * acc_ref[c, mp] + jnp.dot(
                vt, p_ref[c, mp], preferred_element_type=F32)

    lp = lamp_ref[...]
    lam = (jnp.exp(jnp.sum(lp[0:1] * lp[1:2], axis=-1, keepdims=True))
           - jnp.exp(jnp.sum(lp[2:3] * lp[3:4], axis=-1, keepdims=True)) + lam_init)

    def finalize(c):
        acc1 = acc_ref[c, 0]
        acc2 = acc_ref[c, 1]
        o = acc1[:V_DIM] / acc1[V_DIM:V_DIM + 1] - lam * (acc2[:V_DIM] / acc2[V_DIM:V_DIM + 1])
        o = o * lax.rsqrt(jnp.mean(o * o, axis=0, keepdims=True) + RMS_EPS)
        o = o * subg_ref[...] * (1.0 - lam_init)
        o_ref[c * tk:(c + 1) * tk, :] = o.T.astype(o_ref.dtype)

    def block_of(c, t):
        return jnp.where(t <= 0, q_block(c), t - 1)

    def body(t, carry):
        out = []
        for c in range(n_chain):
            ms, scales, maxes = carry[c]
            accumulate(c, block_of(c, t - 1), scales)
            ms, scales = numerators(c, block_of(c, t), ms, maxes)
            maxes = scores(c, t, False)
            out.append((ms, scales, maxes))
        return tuple(out)

    m0 = jnp.full((1, tk), -jnp.inf, F32)
    one = jnp.ones((1, tk), F32)
    init = tuple(((m0, m0), (one, one), scores(c, q_block(c), True)) for c in range(n_chain))
    common = q_block(0)
    state = lax.fori_loop(0, common, body, init)
    for c in range(n_chain):
        ms, scales, maxes = state[c]
        accumulate(c, block_of(c, common - 1), scales)
        for extra in range(c + 1):
            step = common + extra
            kb = block_of(c, step)
            ms, scales = numerators(c, kb, ms, maxes)
            if extra < c:
                maxes = scores(c, step, False)
            accumulate(c, kb, scales)
        finalize(c)


def _attention(qt, k, vt, slopes, lamp, subg, lam_init):
    bsz, _, _, seq = qt.shape
    n = bsz * seq
    tq = ATTN_Q
    nq = seq // tq
    n_chain = ATTN_Q // ATTN_K
    vrows = V_DIM + ONES_ROWS
    return pl.pallas_call(
        functools.partial(_attn_kernel, lam_init),
        grid=(bsz, N_HEADS, nq),
        in_specs=[
            pl.BlockSpec(memory_space=pltpu.SMEM),
            pl.BlockSpec((None, 2, SLAB, tq), lambda b, h, i: (b, 0, h, i)),
            pl.BlockSpec((2, seq, SLAB), lambda b, h, i: (0, b, h)),
            pl.BlockSpec((None, vrows, seq), lambda b, h, i: (b, h, 0)),
            pl.BlockSpec((4, HEAD_DIM), lambda b, h, i: (0, 0)),
            pl.BlockSpec((V_DIM, 1), lambda b, h, i: (0, 0)),
        ],
        out_specs=pl.BlockSpec((tq, V_DIM), lambda b, h, i: (b * nq + i, h)),
        out_shape=jax.ShapeDtypeStruct((n, D_ATTN), BF16),
        scratch_shapes=[
            pltpu.VMEM((n_chain, 2, vrows, ATTN_K), F32),
            pltpu.VMEM((n_chain, 2, ATTN_K, ATTN_K), F32),
            pltpu.VMEM((n_chain, 2, ATTN_K, ATTN_K), BF16),
        ],
        compiler_params=pltpu.CompilerParams(
            dimension_semantics=("arbitrary", "arbitrary", "arbitrary"),
            vmem_limit_bytes=VMEM_LIMIT_BYTES),
        name="diff_attn",
    )(slopes, qt, k, vt, lamp, subg)


def _merge_kernel(x_ref, o_ref, gp_ref, gb_ref, wpa_ref, wo_ref, g_ref, b_ref, out_ref):
    y_attn = jnp.dot(o_ref[...], wpa_ref[...], preferred_element_type=F32)
    mixed = (gp_ref[...] + gb_ref[...] * y_attn).astype(BF16)
    mix = jnp.dot(mixed, wo_ref[...], preferred_element_type=F32)
    out_ref[...] = _layer_norm(DEEPNORM_ALPHA * x_ref[...] + mix, g_ref[...], b_ref[...])


def _merge(x2d, o, gp, gb, wpa, wo, g, b):
    n = x2d.shape[0]
    tm = FFN_ROWS
    row_spec = pl.BlockSpec((tm, D_MODEL), lambda i: (i, 0))
    return pl.pallas_call(
        _merge_kernel,
        grid=(n // tm,),
        in_specs=[row_spec, row_spec, row_spec, row_spec,
                  _resident((D_ATTN, D_MODEL)), _resident((D_MODEL, D_MODEL)),
                  _resident((1, D_MODEL)), _resident((1, D_MODEL))],
        out_specs=row_spec,
        out_shape=jax.ShapeDtypeStruct((n, D_MODEL), F32),
        compiler_params=pltpu.CompilerParams(
            dimension_semantics=("arbitrary",), vmem_limit_bytes=VMEM_LIMIT_BYTES),
        name="merge",
    )(x2d, o, gp, gb, wpa, wo, g, b)


def _slope_pieces(slopes):
    c = slopes * LOG2E
    pieces = []
    for _ in range(N_SPLIT):
        piece = c.astype(BF16).astype(F32)
        pieces.append(piece)
        c = c - piece
    aux = jnp.zeros((N_HEADS, HEAD_DIM), F32)
    aux = aux.at[:, :2 * N_SPLIT].set(jnp.stack(pieces + pieces, axis=1))
    return aux.reshape(D_QK, 1)


def kernel(x, ffn1_w_gate, ffn1_w_up, ffn1_w_down, ln1_g, ln1_b, w_in, b_gate, pool_w, pool_b, pool_scale, lambda_q1, lambda_k1, lambda_q2, lambda_k2, subln_g, w_proj_pool, w_proj_attn, w_out, ln2_g, ln2_b, ffn2_w_gate, ffn2_w_up, ffn2_w_down, ln3_g, ln3_b):
    bsz, seq, d = x.shape
    n = bsz * seq
    h = x.reshape(n, d)
    slopes = jnp.exp2(-8.0 / N_HEADS * jnp.arange(1, N_HEADS + 1, dtype=F32))
    qaux = _slope_pieces(slopes)
    o_q, o_k, o_v, o_g = D_POOL, D_POOL + 2 * D_QK, D_POOL + 4 * D_QK, D_POOL + 4 * D_QK + D_ATTN
    for l in range(DEPTH):
        lam_init = 0.8 - 0.6 * math.exp(-0.3 * l)
        h = _ffn_ln(h, ffn1_w_gate[l].astype(BF16), ffn1_w_up[l].astype(BF16),
                    ffn1_w_down[l].astype(BF16), ln1_g[l][None], ln1_b[l][None])

        wi = w_in[l]
        qt, k, vt, gp, gb = _in_proj(
            h, seq,
            wi[:, :o_q].astype(BF16),
            wi[:, o_q:o_k].T.astype(BF16),
            wi[:, o_k:o_v].astype(BF16),
            wi[:, o_v:o_g].T.astype(BF16),
            wi[:, o_g:].astype(BF16),
            b_gate[l][None],
            qaux,
            pool_w[l].astype(BF16),
            pool_b[l][:, None, :],
            pool_scale[l][None],
            w_proj_pool[l].astype(BF16))

        lamp = jnp.stack([lambda_q1[l], lambda_k1[l], lambda_q2[l], lambda_k2[l]]).astype(F32)
        o = _attention(qt, k, vt, slopes, lamp, subln_g[l][:, None].astype(F32), lam_init)

        h = _merge(h, o, gp, gb, w_proj_attn[l].astype(BF16), w_out[l].astype(BF16),
                   ln2_g[l][None], ln2_b[l][None])

        h = _ffn_ln(h, ffn2_w_gate[l].astype(BF16), ffn2_w_up[l].astype(BF16),
                    ffn2_w_down[l].astype(BF16), ln3_g[l][None], ln3_b[l][None])
    return h.reshape(bsz, seq, d)
```

```python
import functools
import math

import jax
import jax.numpy as jnp
import numpy as np
from jax import lax
from jax.experimental import pallas as pl
from jax.experimental.pallas import tpu as pltpu

F32 = jnp.float32
BF16 = jnp.bfloat16

D_MODEL = 1024
D_FF = 2816
POOL_WINDOWS = (2, 4, 8, 16)
POOL_GROUP_DIM = 128
D_POOL = len(POOL_WINDOWS) * POOL_GROUP_DIM
N_HEADS = 8
HEAD_DIM = 64
V_DIM = 2 * HEAD_DIM
D_ATTN = N_HEADS * V_DIM
D_QK = N_HEADS * HEAD_DIM
LN_EPS = 1e-5
RMS_EPS = 1e-5
DEPTH = 1
DEEPNORM_ALPHA = (2.0 * DEPTH) ** 0.25
LOG2E = math.log2(math.e)
POOL_HISTORY = 16
ONES_ROWS = 16
SLAB = 2 * HEAD_DIM
BF16_EXACT_INT = 256
N_SPLIT = 3

VMEM_LIMIT_BYTES = 56 * 1024 * 1024

FFN_ROWS = 512
MERGE_ROWS = 512
FFN_CHUNK = 256
ATTN_K = 512
N_CHAIN = 2


def _resident(shape):
    nd = len(shape)
    return pl.BlockSpec(shape, lambda *_: (0,) * nd, pipeline_mode=pl.Buffered(1))


def _layer_norm(r, g, b):
    mu = jnp.mean(r, axis=-1, keepdims=True)
    c = r - mu
    var = jnp.mean(c * c, axis=-1, keepdims=True)
    return c * lax.rsqrt(var + LN_EPS) * g + b


def _ffn_ln_block(x, wg_ref, wu_ref, wd_ref, g_ref, b_ref, h_ref):
    xb = x.astype(BF16)
    for c in range(0, D_FF, FFN_CHUNK):
        gate = jnp.dot(xb, wg_ref[:, c:c + FFN_CHUNK], preferred_element_type=F32)
        up = jnp.dot(xb, wu_ref[:, c:c + FFN_CHUNK], preferred_element_type=F32)
        h_ref[:, c:c + FFN_CHUNK] = (gate * jax.nn.sigmoid(gate) * up).astype(BF16)
    y = jnp.dot(h_ref[...], wd_ref[...], preferred_element_type=F32)
    return _layer_norm(DEEPNORM_ALPHA * x + 0.5 * y, g_ref[...], b_ref[...])


def _ffn_ln_kernel(x_ref, wg_ref, wu_ref, wd_ref, g_ref, b_ref, o_ref, h_ref):
    o_ref[...] = _ffn_ln_block(x_ref[...], wg_ref, wu_ref, wd_ref, g_ref, b_ref, h_ref)


def _ffn_ln(x2d, wg, wu, wd, g, b):
    n = x2d.shape[0]
    tm = FFN_ROWS
    return pl.pallas_call(
        _ffn_ln_kernel,
        grid=(n // tm,),
        in_specs=[
            pl.BlockSpec((tm, D_MODEL), lambda i: (i, 0)),
            _resident((D_MODEL, D_FF)),
            _resident((D_MODEL, D_FF)),
            _resident((D_FF, D_MODEL)),
            _resident((1, D_MODEL)),
            _resident((1, D_MODEL)),
        ],
        out_specs=pl.BlockSpec((tm, D_MODEL), lambda i: (i, 0)),
        out_shape=jax.ShapeDtypeStruct((n, D_MODEL), F32),
        scratch_shapes=[pltpu.VMEM((tm, D_FF), BF16)],
        compiler_params=pltpu.CompilerParams(
            dimension_semantics=("arbitrary",), vmem_limit_bytes=VMEM_LIMIT_BYTES),
        name="ffn_ln",
    )(x2d, wg, wu, wd, g, b)


def _in_proj_kernel(tiles_per_seq, x_ref, wpool_ref, wqt_ref, wk_ref, wvt_ref, wg_ref, bg_ref,
                    qaux_ref, poolw_ref, poolb_ref, pscale_ref, wpp_ref,
                    qt_ref, k_ref, vt_ref, gp_ref, gb_ref, ubuf_ref):
    i = pl.program_id(0)
    tm = x_ref.shape[0]
    xb = x_ref[...].astype(BF16)
    seq_tile = i % tiles_per_seq
    nt = (((1,), (1,)), ((), ()))

    qt = lax.dot_general(wqt_ref[...], xb, nt, preferred_element_type=F32)
    qt = (qt * (HEAD_DIM ** -0.5 * LOG2E)).astype(BF16)
    for hd in range(N_HEADS):
        aux = jnp.broadcast_to(qaux_ref[hd * HEAD_DIM:(hd + 1) * HEAD_DIM, :],
                               (HEAD_DIM, tm)).astype(BF16)
        feat_rows = hd * SLAB + (hd % 2) * HEAD_DIM
        aux_rows = hd * SLAB + (1 - hd % 2) * HEAD_DIM
        for mp in range(2):
            src = mp * D_QK + hd * HEAD_DIM
            qt_ref[mp, feat_rows:feat_rows + HEAD_DIM, :] = qt[src:src + HEAD_DIM]
            qt_ref[mp, aux_rows:aux_rows + HEAD_DIM, :] = aux

    k = jnp.dot(xb, wk_ref[...], preferred_element_type=F32)
    row = lax.broadcasted_iota(jnp.int32, (tm, SLAB), 0)
    lane = lax.broadcasted_iota(jnp.int32, (tm, SLAB), 1)
    j = (seq_tile * tm + row) & (ATTN_K - 1)
    jlo = (j & (BF16_EXACT_INT - 1)).astype(F32)
    jhi = (j - (j & (BF16_EXACT_INT - 1))).astype(F32)
    low = lane < HEAD_DIM
    al = lane & (HEAD_DIM - 1)
    aux = jnp.where(al < N_SPLIT, jlo, jnp.where(al < 2 * N_SPLIT, jhi, 0.0))
    for mp in range(2):
        for pair in range(N_HEADS // 2):
            src = mp * D_QK + pair * SLAB
            feats = k[:, src:src + SLAB]
            dst = 2 * pair * SLAB
            k_ref[mp, :, dst:dst + SLAB] = jnp.where(low, feats, aux).astype(BF16)
            k_ref[mp, :, dst + SLAB:dst + 2 * SLAB] = jnp.where(low, aux, feats).astype(BF16)

    vt = lax.dot_general(wvt_ref[...], xb, nt, preferred_element_type=F32)
    vrows = V_DIM + ONES_ROWS
    for hd in range(N_HEADS):
        vt_ref[hd * vrows:hd * vrows + V_DIM, :] = vt[hd * V_DIM:(hd + 1) * V_DIM].astype(BF16)
        vt_ref[hd * vrows + V_DIM:(hd + 1) * vrows, :] = jnp.ones((ONES_ROWS, tm), BF16)

    gates = jax.nn.sigmoid(jnp.dot(xb, wg_ref[...], preferred_element_type=F32) + bg_ref[...])
    gb_ref[...] = gates[:, D_MODEL:]

    hist = POOL_HISTORY
    u = jnp.dot(xb, wpool_ref[...], preferred_element_type=F32)

    @pl.when(seq_tile == 0)
    def _():
        ubuf_ref[0:hist, :] = jnp.zeros((hist, D_POOL), F32)

    @pl.when(seq_tile != 0)
    def _():
        ubuf_ref[0:hist, :] = ubuf_ref[tm:tm + hist, :]

    ubuf_ref[hist:hist + tm, :] = u

    pos = seq_tile * tm + lax.broadcasted_iota(jnp.int32, (tm, 1), 0)
    zs = []
    for g, w in enumerate(POOL_WINDOWS):
        lanes = slice(g * POOL_GROUP_DIM, (g + 1) * POOL_GROUP_DIM)
        ug = u[:, lanes]
        win = ug
        for d in range(1, w):
            win = win + ubuf_ref[hist - d:hist - d + tm, lanes]
        cnt = jnp.minimum(pos + 1, w).astype(F32)
        z = (win / cnt - ug).astype(BF16)
        zg = jnp.dot(z, poolw_ref[g], preferred_element_type=F32) + poolb_ref[g]
        zs.append(zg)
    zall = (jnp.concatenate(zs, axis=-1) * pscale_ref[...]).astype(BF16)
    y_pool = jnp.dot(zall, wpp_ref[...], preferred_element_type=F32)
    gp_ref[...] = gates[:, :D_MODEL] * y_pool


def _in_proj(x2d, seq, wpool, wqt, wk, wvt, wg, bg, qaux, poolw, poolb, pscale, wpp):
    n = x2d.shape[0]
    bsz = n // seq
    tm = FFN_ROWS
    tps = seq // tm
    row_spec = pl.BlockSpec((tm, D_MODEL), lambda i: (i, 0))
    vt_rows = N_HEADS * (V_DIM + ONES_ROWS)
    return pl.pallas_call(
        functools.partial(_in_proj_kernel, tps),
        grid=(n // tm,),
        in_specs=[
            row_spec,
            _resident((D_MODEL, D_POOL)),
            _resident((2 * D_QK, D_MODEL)),
            _resident((D_MODEL, 2 * D_QK)),
            _resident((D_ATTN, D_MODEL)),
            _resident((D_MODEL, 2 * D_MODEL)),
            _resident((1, 2 * D_MODEL)),
            _resident((D_QK, 1)),
            _resident((len(POOL_WINDOWS), POOL_GROUP_DIM, POOL_GROUP_DIM)),
            _resident((len(POOL_WINDOWS), 1, POOL_GROUP_DIM)),
            _resident((1, D_POOL)),
            _resident((D_POOL, D_MODEL)),
        ],
        out_specs=[
            pl.BlockSpec((None, 2, N_HEADS * SLAB, tm), lambda i: (i // tps, 0, 0, i % tps)),
            pl.BlockSpec((2, tm, N_HEADS * SLAB), lambda i: (0, i, 0)),
            pl.BlockSpec((None, vt_rows, tm), lambda i: (i // tps, 0, i % tps)),
            row_spec,
            row_spec,
        ],
        out_shape=[
            jax.ShapeDtypeStruct((bsz, 2, N_HEADS * SLAB, seq), BF16),
            jax.ShapeDtypeStruct((2, n, N_HEADS * SLAB), BF16),
            jax.ShapeDtypeStruct((bsz, vt_rows, seq), BF16),
            jax.ShapeDtypeStruct((n, D_MODEL), F32),
            jax.ShapeDtypeStruct((n, D_MODEL), F32),
        ],
        scratch_shapes=[pltpu.VMEM((POOL_HISTORY + tm, D_POOL), F32)],
        compiler_params=pltpu.CompilerParams(
            dimension_semantics=("arbitrary",), vmem_limit_bytes=VMEM_LIMIT_BYTES),
        name="in_proj",
    )(x2d, wpool, wqt, wk, wvt, wg, bg, qaux, poolw, poolb, pscale, wpp)


SCHED_KB, SCHED_QB, SCHED_FIRST, SCHED_LAST = range(4)


def _chain_schedule(nq, n_chain):
    loads = [0] * n_chain
    blocks = [[] for _ in range(n_chain)]
    for qb in reversed(range(nq)):
        c = loads.index(min(loads))
        blocks[c].append(qb)
        loads[c] += qb + 1
    assert len(set(loads)) == 1, f"query blocks do not split evenly over {n_chain} chains: {loads}"
    table = np.zeros((n_chain, 4, loads[0] + 2), np.int32)
    for c in range(n_chain):
        steps = []
        for qb in blocks[c]:
            steps.append((qb, qb, 1, int(qb == 0)))
            steps += [(kb, qb, 0, int(kb == qb - 1)) for kb in range(qb)]
        first_qb = steps[0][SCHED_QB]
        steps = [(0, first_qb, 0, 0)] + steps + [steps[-1][:2] + (0, 0)]
        table[c] = np.array(steps, np.int32).T
    return table


def _attn_kernel(lam_init, n_steps, sched_ref, slopes_ref, qt_ref, k_ref, vt_ref, lamp_ref, subg_ref,
                 o_ref, acc_ref, s_ref, p_ref, mask_ref):
    h = pl.program_id(1)
    tk = ATTN_K
    n_chain = acc_ref.shape[0]
    slope2 = slopes_ref[h] * LOG2E

    acc_ref[...] = jnp.zeros(acc_ref.shape, F32)
    p_ref[...] = jnp.zeros(p_ref.shape, BF16)
    key = lax.broadcasted_iota(jnp.int32, (tk, tk), 0)
    qry = lax.broadcasted_iota(jnp.int32, (tk, tk), 1)
    mask_ref[0] = jnp.zeros((tk, tk), F32)
    mask_ref[1] = jnp.where(key <= qry, 0.0, -jnp.inf)

    def entry(c, field, g):
        return sched_ref[c, field, g + 1]

    def scores(c, g):
        k0 = pl.multiple_of(entry(c, SCHED_KB, g) * tk, tk)
        q0 = pl.multiple_of(entry(c, SCHED_QB, g) * tk, tk)
        mask = mask_ref[entry(c, SCHED_FIRST, g)]
        maxes = []
        for mp in range(2):
            s = jnp.dot(k_ref[mp, pl.ds(k0, tk), :], qt_ref[mp, :, pl.ds(q0, tk)],
                        preferred_element_type=F32) + mask
            s_ref[c, mp] = s
            maxes.append(jnp.max(s, axis=0, keepdims=True))
        return tuple(maxes)

    def numerators(c, g, ms, maxes):
        off = slope2 * ((entry(c, SCHED_KB, g) - entry(c, SCHED_QB, g)) * tk).astype(F32)
        restart = entry(c, SCHED_FIRST, g) == 1
        new_ms, scales = [], []
        for mp in range(2):
            m_old = jnp.where(restart, -jnp.inf, ms[mp])
            m_new = jnp.maximum(m_old, maxes[mp] + off)
            scales.append(jnp.exp2(m_old - m_new))
            p_ref[c, mp] = jnp.exp2(s_ref[c, mp] - (m_new - off)).astype(BF16)
            new_ms.append(m_new)
        return tuple(new_ms), tuple(scales)

    def accumulate(c, g, scales):
        k0 = pl.multiple_of(entry(c, SCHED_KB, g) * tk, tk)
        vt = vt_ref[:, pl.ds(k0, tk)]
        for mp in range(2):
            acc_ref[c, mp] = scales[mp] * acc_ref[c, mp] + jnp.dot(
                vt, p_ref[c, mp], preferred_element_type=F32)

    lp = lamp_ref[...]
    lam = (jnp.exp(jnp.sum(lp[0:1] * lp[1:2], axis=-1, keepdims=True))
           - jnp.exp(jnp.sum(lp[2:3] * lp[3:4], axis=-1, keepdims=True)) + lam_init)

    def finalize(c, g):
        q0 = pl.multiple_of(entry(c, SCHED_QB, g) * tk, tk)
        acc1 = acc_ref[c, 0]
        acc2 = acc_ref[c, 1]
        o = acc1[:V_DIM] / acc1[V_DIM:V_DIM + 1] - lam * (acc2[:V_DIM] / acc2[V_DIM:V_DIM + 1])
        o = o * lax.rsqrt(jnp.mean(o * o, axis=0, keepdims=True) + RMS_EPS)
        o = o * subg_ref[...] * (1.0 - lam_init)
        o_ref[pl.ds(q0, tk), :] = o.T.astype(o_ref.dtype)

    def body(g, carry):
        out = []
        for c in range(n_chain):
            ms, scales, maxes = carry[c]
            accumulate(c, g - 1, scales)
            ms, scales = numerators(c, g, ms, maxes)
            maxes = scores(c, g + 1)
            out.append((ms, scales, maxes))
        for c in range(n_chain):
            pl.when(entry(c, SCHED_LAST, g - 1) == 1)(functools.partial(finalize, c, g - 1))
        return tuple(out)

    m0 = jnp.full((1, tk), -jnp.inf, F32)
    one = jnp.ones((1, tk), F32)
    init = tuple(((m0, m0), (one, one), scores(c, 0)) for c in range(n_chain))
    state = lax.fori_loop(0, n_steps, body, init)
    for c in range(n_chain):
        accumulate(c, n_steps - 1, state[c][1])
        finalize(c, n_steps - 1)


def _attention(qt, k, vt, slopes, lamp, subg, lam_init):
    bsz, _, _, seq = qt.shape
    n = bsz * seq
    tk = ATTN_K
    vrows = V_DIM + ONES_ROWS
    sched = _chain_schedule(seq // tk, N_CHAIN)
    n_steps = sched.shape[2] - 2
    return pl.pallas_call(
        functools.partial(_attn_kernel, lam_init, n_steps),
        grid=(bsz, N_HEADS),
        in_specs=[
            pl.BlockSpec(memory_space=pltpu.SMEM),
            pl.BlockSpec(memory_space=pltpu.SMEM),
            pl.BlockSpec((None, 2, SLAB, seq), lambda b, h: (b, 0, h, 0)),
            pl.BlockSpec((2, seq, SLAB), lambda b, h: (0, b, h)),
            pl.BlockSpec((None, vrows, seq), lambda b, h: (b, h, 0)),
            pl.BlockSpec((4, HEAD_DIM), lambda b, h: (0, 0)),
            pl.BlockSpec((V_DIM, 1), lambda b, h: (0, 0)),
        ],
        out_specs=pl.BlockSpec((seq, V_DIM), lambda b, h: (b, h)),
        out_shape=jax.ShapeDtypeStruct((n, D_ATTN), BF16),
        scratch_shapes=[
            pltpu.VMEM((N_CHAIN, 2, vrows, tk), F32),
            pltpu.VMEM((N_CHAIN, 2, tk, tk), F32),
            pltpu.VMEM((N_CHAIN, 2, tk, tk), BF16),
            pltpu.VMEM((2, tk, tk), F32),
        ],
        compiler_params=pltpu.CompilerParams(
            dimension_semantics=("arbitrary", "arbitrary"),
            vmem_limit_bytes=VMEM_LIMIT_BYTES),
        name="diff_attn",
    )(jnp.asarray(sched), slopes, qt, k, vt, lamp, subg)


def _merge_ffn_kernel(x_ref, o_ref, gp_ref, gb_ref, wpa_ref, wo_ref, g2_ref, b2_ref,
                      wg_ref, wu_ref, wd_ref, g3_ref, b3_ref, out_ref, h_ref):
    y_attn = jnp.dot(o_ref[...], wpa_ref[...], preferred_element_type=F32)
    mixed = (gp_ref[...] + gb_ref[...] * y_attn).astype(BF16)
    mix = jnp.dot(mixed, wo_ref[...], preferred_element_type=F32)
    x2 = _layer_norm(DEEPNORM_ALPHA * x_ref[...] + mix, g2_ref[...], b2_ref[...])
    out_ref[...] = _ffn_ln_block(x2, wg_ref, wu_ref, wd_ref, g3_ref, b3_ref, h_ref)


def _merge_ffn(x2d, o, gp, gb, wpa, wo, g2, b2, wg, wu, wd, g3, b3):
    n = x2d.shape[0]
    tm = MERGE_ROWS
    row_spec = pl.BlockSpec((tm, D_MODEL), lambda i: (i, 0))
    vec_spec = _resident((1, D_MODEL))
    return pl.pallas_call(
        _merge_ffn_kernel,
        grid=(n // tm,),
        in_specs=[row_spec, row_spec, row_spec, row_spec,
                  _resident((D_ATTN, D_MODEL)), _resident((D_MODEL, D_MODEL)), vec_spec, vec_spec,
                  _resident((D_MODEL, D_FF)), _resident((D_MODEL, D_FF)), _resident((D_FF, D_MODEL)),
                  vec_spec, vec_spec],
        out_specs=row_spec,
        out_shape=jax.ShapeDtypeStruct((n, D_MODEL), F32),
        scratch_shapes=[pltpu.VMEM((tm, D_FF), BF16)],
        compiler_params=pltpu.CompilerParams(
            dimension_semantics=("arbitrary",), vmem_limit_bytes=VMEM_LIMIT_BYTES),
        name="merge_ffn",
    )(x2d, o, gp, gb, wpa, wo, g2, b2, wg, wu, wd, g3, b3)


def _slope_pieces(slopes):
    c = slopes * LOG2E
    pieces = []
    for _ in range(N_SPLIT):
        piece = c.astype(BF16).astype(F32)
        pieces.append(piece)
        c = c - piece
    aux = jnp.zeros((N_HEADS, HEAD_DIM), F32)
    aux = aux.at[:, :2 * N_SPLIT].set(jnp.stack(pieces + pieces, axis=1))
    return aux.reshape(D_QK, 1)


def kernel(x, ffn1_w_gate, ffn1_w_up, ffn1_w_down, ln1_g, ln1_b, w_in, b_gate, pool_w, pool_b, pool_scale, lambda_q1, lambda_k1, lambda_q2, lambda_k2, subln_g, w_proj_pool, w_proj_attn, w_out, ln2_g, ln2_b, ffn2_w_gate, ffn2_w_up, ffn2_w_down, ln3_g, ln3_b):
    bsz, seq, d = x.shape
    n = bsz * seq
    h = x.reshape(n, d)
    slopes = jnp.exp2(-8.0 / N_HEADS * jnp.arange(1, N_HEADS + 1, dtype=F32))
    qaux = _slope_pieces(slopes)
    o_q, o_k, o_v, o_g = D_POOL, D_POOL + 2 * D_QK, D_POOL + 4 * D_QK, D_POOL + 4 * D_QK + D_ATTN
    for l in range(DEPTH):
        lam_init = 0.8 - 0.6 * math.exp(-0.3 * l)
        h = _ffn_ln(h, ffn1_w_gate[l].astype(BF16), ffn1_w_up[l].astype(BF16),
                    ffn1_w_down[l].astype(BF16), ln1_g[l][None], ln1_b[l][None])

        wi = w_in[l]
        qt, k, vt, gp, gb = _in_proj(
            h, seq,
            wi[:, :o_q].astype(BF16),
            wi[:, o_q:o_k].T.astype(BF16),
            wi[:, o_k:o_v].astype(BF16),
            wi[:, o_v:o_g].T.astype(BF16),
            wi[:, o_g:].astype(BF16),
            b_gate[l][None],
            qaux,
            pool_w[l].astype(BF16),
            pool_b[l][:, None, :],
            pool_scale[l][None],
            w_proj_pool[l].astype(BF16))

        lamp = jnp.stack([lambda_q1[l], lambda_k1[l], lambda_q2[l], lambda_k2[l]]).astype(F32)
        o = _attention(qt, k, vt, slopes, lamp, subln_g[l][:, None].astype(F32), lam_init)

        h = _merge_ffn(h, o, gp, gb, w_proj_attn[l].astype(BF16), w_out[l].astype(BF16),
                       ln2_g[l][None], ln2_b[l][None],
                       ffn2_w_gate[l].astype(BF16), ffn2_w_up[l].astype(BF16),
                       ffn2_w_down[l].astype(BF16), ln3_g[l][None], ln3_b[l][None])
    return h.reshape(bsz, seq, d)
```

```python
import functools
import math

import jax
import jax.numpy as jnp
import numpy as np
from jax import lax
from jax.experimental import pallas as pl
from jax.experimental.pallas import tpu as pltpu

F32 = jnp.float32
BF16 = jnp.bfloat16

D_MODEL = 1024
D_FF = 2816
POOL_WINDOWS = (2, 4, 8, 16)
POOL_GROUP_DIM = 128
D_POOL = len(POOL_WINDOWS) * POOL_GROUP_DIM
N_HEADS = 8
HEAD_DIM = 64
V_DIM = 2 * HEAD_DIM
D_ATTN = N_HEADS * V_DIM
D_QK = N_HEADS * HEAD_DIM
LN_EPS = 1e-5
RMS_EPS = 1e-5
DEPTH = 1
DEEPNORM_ALPHA = (2.0 * DEPTH) ** 0.25
LOG2E = math.log2(math.e)
POOL_HISTORY = 16
ONES_ROWS = 16
SLAB = 2 * HEAD_DIM
BF16_EXACT_INT = 256
N_SPLIT = 3

VMEM_LIMIT_BYTES = 56 * 1024 * 1024

FFN_ROWS = 512
MERGE_ROWS = 512
FFN_CHUNK = 256
ATTN_K = 512
N_CHAIN = 2


def _resident(shape):
    nd = len(shape)
    return pl.BlockSpec(shape, lambda *_: (0,) * nd, pipeline_mode=pl.Buffered(1))


def _layer_norm(r, g, b):
    mu = jnp.mean(r, axis=-1, keepdims=True)
    c = r - mu
    var = jnp.mean(c * c, axis=-1, keepdims=True)
    return c * lax.rsqrt(var + LN_EPS) * g + b


def _ffn_ln_block(x, wg_ref, wu_ref, wd_ref, g_ref, b_ref, h_ref):
    xb = x.astype(BF16)
    for c in range(0, D_FF, FFN_CHUNK):
        gate = jnp.dot(xb, wg_ref[:, c:c + FFN_CHUNK], preferred_element_type=F32)
        up = jnp.dot(xb, wu_ref[:, c:c + FFN_CHUNK], preferred_element_type=F32)
        h_ref[:, c:c + FFN_CHUNK] = (gate * jax.nn.sigmoid(gate) * up).astype(BF16)
    y = jnp.dot(h_ref[...], wd_ref[...], preferred_element_type=F32)
    return _layer_norm(DEEPNORM_ALPHA * x + 0.5 * y, g_ref[...], b_ref[...])


def _ffn_ln_kernel(x_ref, wg_ref, wu_ref, wd_ref, g_ref, b_ref, o_ref, h_ref):
    o_ref[...] = _ffn_ln_block(x_ref[...], wg_ref, wu_ref, wd_ref, g_ref, b_ref, h_ref)


def _ffn_ln(x2d, wg, wu, wd, g, b):
    n = x2d.shape[0]
    tm = FFN_ROWS
    return pl.pallas_call(
        _ffn_ln_kernel,
        grid=(n // tm,),
        in_specs=[
            pl.BlockSpec((tm, D_MODEL), lambda i: (i, 0)),
            _resident((D_MODEL, D_FF)),
            _resident((D_MODEL, D_FF)),
            _resident((D_FF, D_MODEL)),
            _resident((1, D_MODEL)),
            _resident((1, D_MODEL)),
        ],
        out_specs=pl.BlockSpec((tm, D_MODEL), lambda i: (i, 0)),
        out_shape=jax.ShapeDtypeStruct((n, D_MODEL), F32),
        scratch_shapes=[pltpu.VMEM((tm, D_FF), BF16)],
        compiler_params=pltpu.CompilerParams(
            dimension_semantics=("arbitrary",), vmem_limit_bytes=VMEM_LIMIT_BYTES),
        name="ffn_ln",
    )(x2d, wg, wu, wd, g, b)


def _in_proj_kernel(tiles_per_seq, x_ref, wpool_ref, wqt_ref, wk_ref, wvt_ref, wg_ref, bg_ref,
                    qaux_ref, poolw_ref, poolb_ref, pscale_ref, wpp_ref,
                    qt_ref, k_ref, vt_ref, gp_ref, gb_ref, ubuf_ref):
    i = pl.program_id(0)
    tm = x_ref.shape[0]
    xb = x_ref[...].astype(BF16)
    seq_tile = i % tiles_per_seq
    nt = (((1,), (1,)), ((), ()))

    qt = lax.dot_general(wqt_ref[...], xb, nt, preferred_element_type=F32)
    qt = (qt * (HEAD_DIM ** -0.5 * LOG2E)).astype(BF16)
    for hd in range(N_HEADS):
        aux = jnp.broadcast_to(qaux_ref[hd * HEAD_DIM:(hd + 1) * HEAD_DIM, :],
                               (HEAD_DIM, tm)).astype(BF16)
        feat_rows = hd * SLAB + (hd % 2) * HEAD_DIM
        aux_rows = hd * SLAB + (1 - hd % 2) * HEAD_DIM
        for mp in range(2):
            src = mp * D_QK + hd * HEAD_DIM
            qt_ref[mp, feat_rows:feat_rows + HEAD_DIM, :] = qt[src:src + HEAD_DIM]
            qt_ref[mp, aux_rows:aux_rows + HEAD_DIM, :] = aux

    k = jnp.dot(xb, wk_ref[...], preferred_element_type=F32)
    row = lax.broadcasted_iota(jnp.int32, (tm, SLAB), 0)
    lane = lax.broadcasted_iota(jnp.int32, (tm, SLAB), 1)
    j = (seq_tile * tm + row) & (ATTN_K - 1)
    jlo = (j & (BF16_EXACT_INT - 1)).astype(F32)
    jhi = (j - (j & (BF16_EXACT_INT - 1))).astype(F32)
    low = lane < HEAD_DIM
    al = lane & (HEAD_DIM - 1)
    aux = jnp.where(al < N_SPLIT, jlo, jnp.where(al < 2 * N_SPLIT, jhi, 0.0))
    for mp in range(2):
        for pair in range(N_HEADS // 2):
            src = mp * D_QK + pair * SLAB
            feats = k[:, src:src + SLAB]
            dst = 2 * pair * SLAB
            k_ref[mp, :, dst:dst + SLAB] = jnp.where(low, feats, aux).astype(BF16)
            k_ref[mp, :, dst + SLAB:dst + 2 * SLAB] = jnp.where(low, aux, feats).astype(BF16)

    vt = lax.dot_general(wvt_ref[...], xb, nt, preferred_element_type=F32)
    vrows = V_DIM + ONES_ROWS
    for hd in range(N_HEADS):
        vt_ref[hd * vrows:hd * vrows + V_DIM, :] = vt[hd * V_DIM:(hd + 1) * V_DIM].astype(BF16)
        vt_ref[hd * vrows + V_DIM:(hd + 1) * vrows, :] = jnp.ones((ONES_ROWS, tm), BF16)

    gates = jax.nn.sigmoid(jnp.dot(xb, wg_ref[...], preferred_element_type=F32) + bg_ref[...])
    gb_ref[...] = gates[:, D_MODEL:]

    hist = POOL_HISTORY
    u = jnp.dot(xb, wpool_ref[...], preferred_element_type=F32)

    @pl.when(seq_tile == 0)
    def _():
        ubuf_ref[0:hist, :] = jnp.zeros((hist, D_POOL), F32)

    @pl.when(seq_tile != 0)
    def _():
        ubuf_ref[0:hist, :] = ubuf_ref[tm:tm + hist, :]

    ubuf_ref[hist:hist + tm, :] = u

    pos = seq_tile * tm + lax.broadcasted_iota(jnp.int32, (tm, 1), 0)
    zs = []
    for g, w in enumerate(POOL_WINDOWS):
        lanes = slice(g * POOL_GROUP_DIM, (g + 1) * POOL_GROUP_DIM)
        ug = u[:, lanes]
        win = ug
        for d in range(1, w):
            win = win + ubuf_ref[hist - d:hist - d + tm, lanes]
        cnt = jnp.minimum(pos + 1, w).astype(F32)
        z = (win / cnt - ug).astype(BF16)
        zg = jnp.dot(z, poolw_ref[g], preferred_element_type=F32) + poolb_ref[g]
        zs.append(zg)
    zall = (jnp.concatenate(zs, axis=-1) * pscale_ref[...]).astype(BF16)
    y_pool = jnp.dot(zall, wpp_ref[...], preferred_element_type=F32)
    gp_ref[...] = gates[:, :D_MODEL] * y_pool


def _in_proj(x2d, seq, wpool, wqt, wk, wvt, wg, bg, qaux, poolw, poolb, pscale, wpp):
    n = x2d.shape[0]
    bsz = n // seq
    tm = FFN_ROWS
    tps = seq // tm
    row_spec = pl.BlockSpec((tm, D_MODEL), lambda i: (i, 0))
    vt_rows = N_HEADS * (V_DIM + ONES_ROWS)
    return pl.pallas_call(
        functools.partial(_in_proj_kernel, tps),
        grid=(n // tm,),
        in_specs=[
            row_spec,
            _resident((D_MODEL, D_POOL)),
            _resident((2 * D_QK, D_MODEL)),
            _resident((D_MODEL, 2 * D_QK)),
            _resident((D_ATTN, D_MODEL)),
            _resident((D_MODEL, 2 * D_MODEL)),
            _resident((1, 2 * D_MODEL)),
            _resident((D_QK, 1)),
            _resident((len(POOL_WINDOWS), POOL_GROUP_DIM, POOL_GROUP_DIM)),
            _resident((len(POOL_WINDOWS), 1, POOL_GROUP_DIM)),
            _resident((1, D_POOL)),
            _resident((D_POOL, D_MODEL)),
        ],
        out_specs=[
            pl.BlockSpec((None, 2, N_HEADS * SLAB, tm), lambda i: (i // tps, 0, 0, i % tps)),
            pl.BlockSpec((2, tm, N_HEADS * SLAB), lambda i: (0, i, 0)),
            pl.BlockSpec((None, vt_rows, tm), lambda i: (i // tps, 0, i % tps)),
            row_spec,
            row_spec,
        ],
        out_shape=[
            jax.ShapeDtypeStruct((bsz, 2, N_HEADS * SLAB, seq), BF16),
            jax.ShapeDtypeStruct((2, n, N_HEADS * SLAB), BF16),
            jax.ShapeDtypeStruct((bsz, vt_rows, seq), BF16),
            jax.ShapeDtypeStruct((n, D_MODEL), F32),
            jax.ShapeDtypeStruct((n, D_MODEL), F32),
        ],
        scratch_shapes=[pltpu.VMEM((POOL_HISTORY + tm, D_POOL), F32)],
        compiler_params=pltpu.CompilerParams(
            dimension_semantics=("arbitrary",), vmem_limit_bytes=VMEM_LIMIT_BYTES),
        name="in_proj",
    )(x2d, wpool, wqt, wk, wvt, wg, bg, qaux, poolw, poolb, pscale, wpp)


SCHED_KB, SCHED_QB, SCHED_SLOT, SCHED_FIRST, SCHED_LAST = range(5)


def _chain_schedule(nq, n_chain):
    loads = [0] * n_chain
    blocks = [[] for _ in range(n_chain)]
    for qb in reversed(range(nq)):
        c = loads.index(min(loads))
        blocks[c].append(qb)
        loads[c] += qb + 1
    assert len(set(loads)) == 1 and len({len(b) for b in blocks}) == 1, (
        f"query blocks do not split evenly over {n_chain} chains: {blocks}")
    n_full = loads[0] - len(blocks[0])
    diag = np.zeros((n_chain, len(blocks[0]) + 2), np.int32)
    full = np.zeros((n_chain, 5, n_full + 2), np.int32)
    for c in range(n_chain):
        diag[c] = [blocks[c][0]] + blocks[c] + [blocks[c][-1]]
        steps = []
        for slot, qb in enumerate(blocks[c]):
            steps += [(kb, qb, slot, int(kb == 0), int(kb == qb - 1)) for kb in range(qb)]
        steps = [steps[0][:3] + (0, 0)] + steps + [steps[-1][:3] + (0, 0)]
        full[c] = np.array(steps, np.int32).T
    return blocks, diag, full


def _attn_kernel(lam_init, blocks, n_full, diag_ref, full_ref, slopes_ref, qt_ref, k_ref, vt_ref,
                 lamp_ref, subg_ref, o_ref, acc_ref, m_ref, s_ref, p_ref):
    h = pl.program_id(1)
    tk = ATTN_K
    n_chain = len(blocks)
    n_slots = len(blocks[0])
    slope2 = slopes_ref[h] * LOG2E

    lp = lamp_ref[...]
    lam = (jnp.exp(jnp.sum(lp[0:1] * lp[1:2], axis=-1, keepdims=True))
           - jnp.exp(jnp.sum(lp[2:3] * lp[3:4], axis=-1, keepdims=True)) + lam_init)

    def finalize(c, slot, qb):
        q0 = pl.multiple_of(qb * tk, tk)
        acc1 = acc_ref[c, slot, 0]
        acc2 = acc_ref[c, slot, 1]
        o = acc1[:V_DIM] / acc1[V_DIM:V_DIM + 1] - lam * (acc2[:V_DIM] / acc2[V_DIM:V_DIM + 1])
        o = o * lax.rsqrt(jnp.mean(o * o, axis=0, keepdims=True) + RMS_EPS)
        o = o * subg_ref[...] * (1.0 - lam_init)
        o_ref[pl.ds(q0, tk), :] = o.T.astype(o_ref.dtype)


    key = lax.broadcasted_iota(jnp.int32, (tk, tk), 0)
    qry = lax.broadcasted_iota(jnp.int32, (tk, tk), 1)

    def diag_block(c, j):
        return pl.multiple_of(diag_ref[c, j + 1] * tk, tk)

    def diag_scores(c, j):
        b0 = diag_block(c, j)
        maxes = []
        for mp in range(2):
            s = jnp.dot(k_ref[mp, pl.ds(b0, tk), :], qt_ref[mp, :, pl.ds(b0, tk)],
                        preferred_element_type=F32)
            s = jnp.where(key <= qry, s, -jnp.inf)
            s_ref[c, mp] = s
            maxes.append(jnp.max(s, axis=0, keepdims=True))
        return tuple(maxes)

    def diag_numerators(c, j, maxes):
        for mp in range(2):
            m_ref[c, j, mp] = maxes[mp]
            p_ref[c, mp] = jnp.exp2(s_ref[c, mp] - maxes[mp]).astype(BF16)

    def diag_accumulate(c, j):
        vt = vt_ref[:, pl.ds(diag_block(c, j), tk)]
        for mp in range(2):
            acc_ref[c, j, mp] = jnp.dot(vt, p_ref[c, mp], preferred_element_type=F32)

    def diag_body(j, carry):
        out = []
        for c in range(n_chain):
            diag_accumulate(c, j - 1)
            diag_numerators(c, j, carry[c])
            out.append(diag_scores(c, j + 1))
        return tuple(out)

    init = []
    for c in range(n_chain):
        diag_numerators(c, 0, diag_scores(c, 0))
        init.append(diag_scores(c, 1))
    lax.fori_loop(1, n_slots, diag_body, tuple(init))
    for c in range(n_chain):
        diag_accumulate(c, n_slots - 1)
    for c in range(n_chain):
        if 0 in blocks[c]:
            finalize(c, blocks[c].index(0), 0)


    def entry(c, field, g):
        return full_ref[c, field, g + 1]

    def scores(c, g):
        k0 = pl.multiple_of(entry(c, SCHED_KB, g) * tk, tk)
        q0 = pl.multiple_of(entry(c, SCHED_QB, g) * tk, tk)
        maxes = []
        for mp in range(2):
            s = jnp.dot(k_ref[mp, pl.ds(k0, tk), :], qt_ref[mp, :, pl.ds(q0, tk)],
                        preferred_element_type=F32)
            s_ref[c, mp] = s
            maxes.append(jnp.max(s, axis=0, keepdims=True))
        return tuple(maxes)

    def numerators(c, g, ms, maxes):
        off = slope2 * ((entry(c, SCHED_KB, g) - entry(c, SCHED_QB, g)) * tk).astype(F32)
        resume = entry(c, SCHED_FIRST, g) == 1
        slot = entry(c, SCHED_SLOT, g)
        new_ms, scales = [], []
        for mp in range(2):
            m_old = jnp.where(resume, m_ref[c, slot, mp], ms[mp])
            m_new = jnp.maximum(m_old, maxes[mp] + off)
            scales.append(jnp.exp2(m_old - m_new))
            p_ref[c, mp] = jnp.exp2(s_ref[c, mp] - (m_new - off)).astype(BF16)
            new_ms.append(m_new)
        return tuple(new_ms), tuple(scales)

    def accumulate(c, g, scales):
        k0 = pl.multiple_of(entry(c, SCHED_KB, g) * tk, tk)
        slot = entry(c, SCHED_SLOT, g)
        vt = vt_ref[:, pl.ds(k0, tk)]
        for mp in range(2):
            acc_ref[c, slot, mp] = scales[mp] * acc_ref[c, slot, mp] + jnp.dot(
                vt, p_ref[c, mp], preferred_element_type=F32)

    def finalize_step(c, g):
        finalize(c, entry(c, SCHED_SLOT, g), entry(c, SCHED_QB, g))

    def body(g, carry):
        out = []
        for c in range(n_chain):
            ms, scales, maxes = carry[c]
            accumulate(c, g - 1, scales)
            ms, scales = numerators(c, g, ms, maxes)
            maxes = scores(c, g + 1)
            out.append((ms, scales, maxes))
        for c in range(n_chain):
            pl.when(entry(c, SCHED_LAST, g - 1) == 1)(functools.partial(finalize_step, c, g - 1))
        return tuple(out)

    zero = jnp.zeros((1, tk), F32)
    init = []
    for c in range(n_chain):
        ms, scales = numerators(c, 0, (zero, zero), scores(c, 0))
        init.append((ms, scales, scores(c, 1)))
    state = lax.fori_loop(1, n_full, body, tuple(init))
    for c in range(n_chain):
        accumulate(c, n_full - 1, state[c][1])
        finalize_step(c, n_full - 1)


def _attention(qt, k, vt, slopes, lamp, subg, lam_init):
    bsz, _, _, seq = qt.shape
    n = bsz * seq
    tk = ATTN_K
    vrows = V_DIM + ONES_ROWS
    blocks, diag, full = _chain_schedule(seq // tk, N_CHAIN)
    n_slots = len(blocks[0])
    n_full = full.shape[2] - 2
    return pl.pallas_call(
        functools.partial(_attn_kernel, lam_init, blocks, n_full),
        grid=(bsz, N_HEADS),
        in_specs=[
            pl.BlockSpec(memory_space=pltpu.SMEM),
            pl.BlockSpec(memory_space=pltpu.SMEM),
            pl.BlockSpec(memory_space=pltpu.SMEM),
            pl.BlockSpec((None, 2, SLAB, seq), lambda b, h: (b, 0, h, 0)),
            pl.BlockSpec((2, seq, SLAB), lambda b, h: (0, b, h)),
            pl.BlockSpec((None, vrows, seq), lambda b, h: (b, h, 0)),
            pl.BlockSpec((4, HEAD_DIM), lambda b, h: (0, 0)),
            pl.BlockSpec((V_DIM, 1), lambda b, h: (0, 0)),
        ],
        out_specs=pl.BlockSpec((seq, V_DIM), lambda b, h: (b, h)),
        out_shape=jax.ShapeDtypeStruct((n, D_ATTN), BF16),
        scratch_shapes=[
            pltpu.VMEM((N_CHAIN, n_slots, 2, vrows, tk), F32),
            pltpu.VMEM((N_CHAIN, n_slots, 2, 1, tk), F32),
            pltpu.VMEM((N_CHAIN, 2, tk, tk), F32),
            pltpu.VMEM((N_CHAIN, 2, tk, tk), BF16),
        ],
        compiler_params=pltpu.CompilerParams(
            dimension_semantics=("arbitrary", "arbitrary"),
            vmem_limit_bytes=VMEM_LIMIT_BYTES),
        name="diff_attn",
    )(jnp.asarray(diag), jnp.asarray(full), slopes, qt, k, vt, lamp, subg)


def _merge_ffn_kernel(x_ref, o_ref, gp_ref, gb_ref, wpa_ref, wo_ref, g2_ref, b2_ref,
                      wg_ref, wu_ref, wd_ref, g3_ref, b3_ref, out_ref, h_ref):
    y_attn = jnp.dot(o_ref[...], wpa_ref[...], preferred_element_type=F32)
    mixed = (gp_ref[...] + gb_ref[...] * y_attn).astype(BF16)
    mix = jnp.dot(mixed, wo_ref[...], preferred_element_type=F32)
    x2 = _layer_norm(DEEPNORM_ALPHA * x_ref[...] + mix, g2_ref[...], b2_ref[...])
    out_ref[...] = _ffn_ln_block(x2, wg_ref, wu_ref, wd_ref, g3_ref, b3_ref, h_ref)


def _merge_ffn(x2d, o, gp, gb, wpa, wo, g2, b2, wg, wu, wd, g3, b3):
    n = x2d.shape[0]
    tm = MERGE_ROWS
    row_spec = pl.BlockSpec((tm, D_MODEL), lambda i: (i, 0))
    vec_spec = _resident((1, D_MODEL))
    return pl.pallas_call(
        _merge_ffn_kernel,
        grid=(n // tm,),
        in_specs=[row_spec, row_spec, row_spec, row_spec,
                  _resident((D_ATTN, D_MODEL)), _resident((D_MODEL, D_MODEL)), vec_spec, vec_spec,
                  _resident((D_MODEL, D_FF)), _resident((D_MODEL, D_FF)), _resident((D_FF, D_MODEL)),
                  vec_spec, vec_spec],
        out_specs=row_spec,
        out_shape=jax.ShapeDtypeStruct((n, D_MODEL), F32),
        scratch_shapes=[pltpu.VMEM((tm, D_FF), BF16)],
        compiler_params=pltpu.CompilerParams(
            dimension_semantics=("arbitrary",), vmem_limit_bytes=VMEM_LIMIT_BYTES),
        name="merge_ffn",
    )(x2d, o, gp, gb, wpa, wo, g2, b2, wg, wu, wd, g3, b3)


def _slope_pieces(slopes):
    c = slopes * LOG2E
    pieces = []
    for _ in range(N_SPLIT):
        piece = c.astype(BF16).astype(F32)
        pieces.append(piece)
        c = c - piece
    aux = jnp.zeros((N_HEADS, HEAD_DIM), F32)
    aux = aux.at[:, :2 * N_SPLIT].set(jnp.stack(pieces + pieces, axis=1))
    return aux.reshape(D_QK, 1)


def kernel(x, ffn1_w_gate, ffn1_w_up, ffn1_w_down, ln1_g, ln1_b, w_in, b_gate, pool_w, pool_b, pool_scale, lambda_q1, lambda_k1, lambda_q2, lambda_k2, subln_g, w_proj_pool, w_proj_attn, w_out, ln2_g, ln2_b, ffn2_w_gate, ffn2_w_up, ffn2_w_down, ln3_g, ln3_b):
    bsz, seq, d = x.shape
    n = bsz * seq
    h = x.reshape(n, d)
    slopes = jnp.exp2(-8.0 / N_HEADS * jnp.arange(1, N_HEADS + 1, dtype=F32))
    qaux = _slope_pieces(slopes)
    o_q, o_k, o_v, o_g = D_POOL, D_POOL + 2 * D_QK, D_POOL + 4 * D_QK, D_POOL + 4 * D_QK + D_ATTN
    for l in range(DEPTH):
        lam_init = 0.8 - 0.6 * math.exp(-0.3 * l)
        h = _ffn_ln(h, ffn1_w_gate[l].astype(BF16), ffn1_w_up[l].astype(BF16),
                    ffn1_w_down[l].astype(BF16), ln1_g[l][None], ln1_b[l][None])

        wi = w_in[l]
        qt, k, vt, gp, gb = _in_proj(
            h, seq,
            wi[:, :o_q].astype(BF16),
            wi[:, o_q:o_k].T.astype(BF16),
            wi[:, o_k:o_v].astype(BF16),
            wi[:, o_v:o_g].T.astype(BF16),
            wi[:, o_g:].astype(BF16),
            b_gate[l][None],
            qaux,
            pool_w[l].astype(BF16),
            pool_b[l][:, None, :],
            pool_scale[l][None],
            w_proj_pool[l].astype(BF16))

        lamp = jnp.stack([lambda_q1[l], lambda_k1[l], lambda_q2[l], lambda_k2[l]]).astype(F32)
        o = _attention(qt, k, vt, slopes, lamp, subln_g[l][:, None].astype(F32), lam_init)

        h = _merge_ffn(h, o, gp, gb, w_proj_attn[l].astype(BF16), w_out[l].astype(BF16),
                       ln2_g[l][None], ln2_b[l][None],
                       ffn2_w_gate[l].astype(BF16), ffn2_w_up[l].astype(BF16),
                       ffn2_w_down[l].astype(BF16), ln3_g[l][None], ln3_b[l][None])
    return h.reshape(bsz, seq, d)
```

```python
import functools
import math

import jax
import jax.numpy as jnp
import numpy as np
from jax import lax
from jax.experimental import pallas as pl
from jax.experimental.pallas import tpu as pltpu

F32 = jnp.float32
BF16 = jnp.bfloat16

D_MODEL = 1024
D_FF = 2816
POOL_WINDOWS = (2, 4, 8, 16)
POOL_GROUP_DIM = 128
D_POOL = len(POOL_WINDOWS) * POOL_GROUP_DIM
N_HEADS = 8
HEAD_DIM = 64
V_DIM = 2 * HEAD_DIM
D_ATTN = N_HEADS * V_DIM
D_QK = N_HEADS * HEAD_DIM
LN_EPS = 1e-5
RMS_EPS = 1e-5
DEPTH = 1
DEEPNORM_ALPHA = (2.0 * DEPTH) ** 0.25
LOG2E = math.log2(math.e)
SUBLANES = 8
POOL_HISTORY = SUBLANES * (max(POOL_WINDOWS).bit_length() - 1)
ONES_ROWS = 16
SLAB = 2 * HEAD_DIM
BF16_EXACT_INT = 256
N_SPLIT = 3

VMEM_LIMIT_BYTES = 56 * 1024 * 1024

FFN_ROWS = 512
MERGE_ROWS = 512
FFN_CHUNK = 256
ATTN_K = 512
N_CHAIN = 2


def _resident(shape):
    nd = len(shape)
    return pl.BlockSpec(shape, lambda *_: (0,) * nd, pipeline_mode=pl.Buffered(1))


def _layer_norm(r, g, b):
    mu = jnp.mean(r, axis=-1, keepdims=True)
    c = r - mu
    var = jnp.mean(c * c, axis=-1, keepdims=True)
    return c * lax.rsqrt(var + LN_EPS) * g + b


def _ffn_ln_block(x, wg_ref, wu_ref, wd_ref, g_ref, b_ref, h_ref):
    xb = x.astype(BF16)
    for c in range(0, D_FF, FFN_CHUNK):
        gate = jnp.dot(xb, wg_ref[:, c:c + FFN_CHUNK], preferred_element_type=F32)
        up = jnp.dot(xb, wu_ref[:, c:c + FFN_CHUNK], preferred_element_type=F32)
        h_ref[:, c:c + FFN_CHUNK] = (gate * jax.nn.sigmoid(gate) * up).astype(BF16)
    y = jnp.dot(h_ref[...], wd_ref[...], preferred_element_type=F32)
    return _layer_norm(DEEPNORM_ALPHA * x + 0.5 * y, g_ref[...], b_ref[...])


def _ffn_ln_kernel(x_ref, wg_ref, wu_ref, wd_ref, g_ref, b_ref, o_ref, h_ref):
    o_ref[...] = _ffn_ln_block(x_ref[...], wg_ref, wu_ref, wd_ref, g_ref, b_ref, h_ref)


def _ffn_ln(x2d, wg, wu, wd, g, b):
    n = x2d.shape[0]
    tm = FFN_ROWS
    return pl.pallas_call(
        _ffn_ln_kernel,
        grid=(n // tm,),
        in_specs=[
            pl.BlockSpec((tm, D_MODEL), lambda i: (i, 0)),
            _resident((D_MODEL, D_FF)),
            _resident((D_MODEL, D_FF)),
            _resident((D_FF, D_MODEL)),
            _resident((1, D_MODEL)),
            _resident((1, D_MODEL)),
        ],
        out_specs=pl.BlockSpec((tm, D_MODEL), lambda i: (i, 0)),
        out_shape=jax.ShapeDtypeStruct((n, D_MODEL), F32),
        scratch_shapes=[pltpu.VMEM((tm, D_FF), BF16)],
        compiler_params=pltpu.CompilerParams(
            dimension_semantics=("arbitrary",), vmem_limit_bytes=VMEM_LIMIT_BYTES),
        name="ffn_ln",
    )(x2d, wg, wu, wd, g, b)


def _trailing_window_sum(ubuf_ref, wbuf_ref, lanes, w, tm):
    end = POOL_HISTORY + tm
    read = lambda a, b: ubuf_ref[a:b, lanes]
    for k in range(1, w.bit_length()):
        span, start = w >> k, k * SUBLANES
        level = read(start, end) + read(start - span, end - span)
        if span == 1:
            return level[POOL_HISTORY - start:]
        dst = wbuf_ref.at[k % 2]
        dst[start:end, :] = level
        read = lambda a, b, dst=dst: dst[a:b, :]


def _in_proj_kernel(tiles_per_seq, x_ref, wpool_ref, wqt_ref, wk_ref, wvt_ref, wg_ref, bg_ref,
                    qaux_ref, poolw_ref, poolb_ref, pscale_ref, wpp_ref,
                    qt_ref, k_ref, vt_ref, gp_ref, gb_ref, ubuf_ref, wbuf_ref):
    i = pl.program_id(0)
    tm = x_ref.shape[0]
    xb = x_ref[...].astype(BF16)
    seq_tile = i % tiles_per_seq
    nt = (((1,), (1,)), ((), ()))

    qt = lax.dot_general(wqt_ref[...], xb, nt, preferred_element_type=F32)
    qt = (qt * (HEAD_DIM ** -0.5 * LOG2E)).astype(BF16)
    for hd in range(N_HEADS):
        aux = jnp.broadcast_to(qaux_ref[hd * HEAD_DIM:(hd + 1) * HEAD_DIM, :],
                               (HEAD_DIM, tm)).astype(BF16)
        feat_rows = hd * SLAB + (hd % 2) * HEAD_DIM
        aux_rows = hd * SLAB + (1 - hd % 2) * HEAD_DIM
        for mp in range(2):
            src = mp * D_QK + hd * HEAD_DIM
            qt_ref[mp, feat_rows:feat_rows + HEAD_DIM, :] = qt[src:src + HEAD_DIM]
            qt_ref[mp, aux_rows:aux_rows + HEAD_DIM, :] = aux

    k = jnp.dot(xb, wk_ref[...], preferred_element_type=F32)
    row = lax.broadcasted_iota(jnp.int32, (tm, SLAB), 0)
    lane = lax.broadcasted_iota(jnp.int32, (tm, SLAB), 1)
    j = (seq_tile * tm + row) & (ATTN_K - 1)
    jlo = (j & (BF16_EXACT_INT - 1)).astype(F32)
    jhi = (j - (j & (BF16_EXACT_INT - 1))).astype(F32)
    low = lane < HEAD_DIM
    al = lane & (HEAD_DIM - 1)
    aux = jnp.where(al < N_SPLIT, jlo, jnp.where(al < 2 * N_SPLIT, jhi, 0.0))
    for mp in range(2):
        for pair in range(N_HEADS // 2):
            src = mp * D_QK + pair * SLAB
            feats = k[:, src:src + SLAB]
            dst = 2 * pair * SLAB
            k_ref[mp, :, dst:dst + SLAB] = jnp.where(low, feats, aux).astype(BF16)
            k_ref[mp, :, dst + SLAB:dst + 2 * SLAB] = jnp.where(low, aux, feats).astype(BF16)

    vt = lax.dot_general(wvt_ref[...], xb, nt, preferred_element_type=F32)
    vrows = V_DIM + ONES_ROWS
    for hd in range(N_HEADS):
        vt_ref[hd * vrows:hd * vrows + V_DIM, :] = vt[hd * V_DIM:(hd + 1) * V_DIM].astype(BF16)
        vt_ref[hd * vrows + V_DIM:(hd + 1) * vrows, :] = jnp.ones((ONES_ROWS, tm), BF16)

    gates = jax.nn.sigmoid(jnp.dot(xb, wg_ref[...], preferred_element_type=F32) + bg_ref[...])
    gb_ref[...] = gates[:, D_MODEL:]

    hist = POOL_HISTORY
    u = jnp.dot(xb, wpool_ref[...], preferred_element_type=F32)

    @pl.when(seq_tile == 0)
    def _():
        ubuf_ref[0:hist, :] = jnp.zeros((hist, D_POOL), F32)

    @pl.when(seq_tile != 0)
    def _():
        ubuf_ref[0:hist, :] = ubuf_ref[tm:tm + hist, :]

    ubuf_ref[hist:hist + tm, :] = u

    pos = seq_tile * tm + lax.broadcasted_iota(jnp.int32, (tm, 1), 0)
    zs = []
    for g, w in enumerate(POOL_WINDOWS):
        lanes = slice(g * POOL_GROUP_DIM, (g + 1) * POOL_GROUP_DIM)
        ug = u[:, lanes]
        win = _trailing_window_sum(ubuf_ref, wbuf_ref, lanes, w, tm)
        cnt = jnp.minimum(pos + 1, w).astype(F32)
        z = (win / cnt - ug).astype(BF16)
        zg = jnp.dot(z, poolw_ref[g], preferred_element_type=F32) + poolb_ref[g]
        zs.append(zg)
    zall = (jnp.concatenate(zs, axis=-1) * pscale_ref[...]).astype(BF16)
    y_pool = jnp.dot(zall, wpp_ref[...], preferred_element_type=F32)
    gp_ref[...] = gates[:, :D_MODEL] * y_pool


def _in_proj(x2d, seq, wpool, wqt, wk, wvt, wg, bg, qaux, poolw, poolb, pscale, wpp):
    n = x2d.shape[0]
    bsz = n // seq
    tm = FFN_ROWS
    tps = seq // tm
    row_spec = pl.BlockSpec((tm, D_MODEL), lambda i: (i, 0))
    vt_rows = N_HEADS * (V_DIM + ONES_ROWS)
    return pl.pallas_call(
        functools.partial(_in_proj_kernel, tps),
        grid=(n // tm,),
        in_specs=[
            row_spec,
            _resident((D_MODEL, D_POOL)),
            _resident((2 * D_QK, D_MODEL)),
            _resident((D_MODEL, 2 * D_QK)),
            _resident((D_ATTN, D_MODEL)),
            _resident((D_MODEL, 2 * D_MODEL)),
            _resident((1, 2 * D_MODEL)),
            _resident((D_QK, 1)),
            _resident((len(POOL_WINDOWS), POOL_GROUP_DIM, POOL_GROUP_DIM)),
            _resident((len(POOL_WINDOWS), 1, POOL_GROUP_DIM)),
            _resident((1, D_POOL)),
            _resident((D_POOL, D_MODEL)),
        ],
        out_specs=[
            pl.BlockSpec((None, 2, N_HEADS * SLAB, tm), lambda i: (i // tps, 0, 0, i % tps)),
            pl.BlockSpec((2, tm, N_HEADS * SLAB), lambda i: (0, i, 0)),
            pl.BlockSpec((None, vt_rows, tm), lambda i: (i // tps, 0, i % tps)),
            row_spec,
            row_spec,
        ],
        out_shape=[
            jax.ShapeDtypeStruct((bsz, 2, N_HEADS * SLAB, seq), BF16),
            jax.ShapeDtypeStruct((2, n, N_HEADS * SLAB), BF16),
            jax.ShapeDtypeStruct((bsz, vt_rows, seq), BF16),
            jax.ShapeDtypeStruct((n, D_MODEL), F32),
            jax.ShapeDtypeStruct((n, D_MODEL), F32),
        ],
        scratch_shapes=[pltpu.VMEM((POOL_HISTORY + tm, D_POOL), F32),
                        pltpu.VMEM((2, POOL_HISTORY + tm, POOL_GROUP_DIM), F32)],
        compiler_params=pltpu.CompilerParams(
            dimension_semantics=("arbitrary",), vmem_limit_bytes=VMEM_LIMIT_BYTES),
        name="in_proj",
    )(x2d, wpool, wqt, wk, wvt, wg, bg, qaux, poolw, poolb, pscale, wpp)


SCHED_KB, SCHED_QB, SCHED_SLOT, SCHED_FIRST, SCHED_LAST = range(5)
N_SCHED_FIELDS = 5


def _chain_schedule(nq, n_chain):
    loads = [0] * n_chain
    blocks = [[] for _ in range(n_chain)]
    for qb in reversed(range(nq)):
        c = loads.index(min(loads))
        blocks[c].append(qb)
        loads[c] += qb + 1
    assert len(set(loads)) == 1 and len({len(b) for b in blocks}) == 1, (
        f"query blocks do not split evenly over {n_chain} chains: {blocks}")
    n_full = loads[0] - len(blocks[0])
    diag = np.zeros((n_chain, len(blocks[0]) + 2), np.int32)
    full = np.zeros((n_chain, N_SCHED_FIELDS, n_full + 2), np.int32)
    for c in range(n_chain):
        diag[c] = [blocks[c][0]] + blocks[c] + [blocks[c][-1]]
        steps = []
        for slot, qb in enumerate(blocks[c]):
            steps += [(kb, qb, slot, int(kb == 0), int(kb == qb - 1)) for kb in range(qb)]
        steps = [steps[0][:3] + (0, 0)] + steps + [steps[-1][:3] + (0, 0)]
        full[c] = np.array(steps, np.int32).T
    return blocks, diag, full


def _attn_kernel(lam_init, blocks, n_full, diag_ref, full_ref, slopes_ref, qt_ref, k_ref, vt_ref,
                 lamp_ref, subg_ref, o_ref, acc_ref, m_ref, s_ref, p_ref):
    h = pl.program_id(1)
    tk = ATTN_K
    n_chain = len(blocks)
    n_slots = len(blocks[0])
    slope2 = slopes_ref[h] * LOG2E

    lp = lamp_ref[...]
    lam = (jnp.exp(jnp.sum(lp[0:1] * lp[1:2], axis=-1, keepdims=True))
           - jnp.exp(jnp.sum(lp[2:3] * lp[3:4], axis=-1, keepdims=True)) + lam_init)

    def finalize(c, slot, qb):
        q0 = pl.multiple_of(qb * tk, tk)
        acc1 = acc_ref[c, slot, 0]
        acc2 = acc_ref[c, slot, 1]
        o = acc1[:V_DIM] / acc1[V_DIM:V_DIM + 1] - lam * (acc2[:V_DIM] / acc2[V_DIM:V_DIM + 1])
        o = o * lax.rsqrt(jnp.mean(o * o, axis=0, keepdims=True) + RMS_EPS)
        o = o * subg_ref[...] * (1.0 - lam_init)
        o_ref[pl.ds(q0, tk), :] = o.T.astype(o_ref.dtype)


    key = lax.broadcasted_iota(jnp.int32, (tk, tk), 0)
    qry = lax.broadcasted_iota(jnp.int32, (tk, tk), 1)

    def diag_block(c, j):
        return pl.multiple_of(diag_ref[c, j + 1] * tk, tk)

    def diag_scores(c, j):
        b0 = diag_block(c, j)
        maxes = []
        for mp in range(2):
            s = jnp.dot(k_ref[mp, pl.ds(b0, tk), :], qt_ref[mp, :, pl.ds(b0, tk)],
                        preferred_element_type=F32)
            s = jnp.where(key <= qry, s, -jnp.inf)
            s_ref[c, mp] = s
            maxes.append(jnp.max(s, axis=0, keepdims=True))
        return tuple(maxes)

    def diag_numerators(c, j, maxes):
        for mp in range(2):
            m_ref[c, j, mp] = maxes[mp]
            p_ref[c, mp] = jnp.exp2(s_ref[c, mp] - maxes[mp]).astype(BF16)

    def diag_accumulate(c, j):
        vt = vt_ref[:, pl.ds(diag_block(c, j), tk)]
        for mp in range(2):
            acc_ref[c, j, mp] = jnp.dot(vt, p_ref[c, mp], preferred_element_type=F32)

    def diag_body(j, carry):
        out = []
        for c in range(n_chain):
            diag_accumulate(c, j - 1)
            diag_numerators(c, j, carry[c])
            out.append(diag_scores(c, j + 1))
        return tuple(out)

    init = []
    for c in range(n_chain):
        diag_numerators(c, 0, diag_scores(c, 0))
        init.append(diag_scores(c, 1))
    lax.fori_loop(1, n_slots, diag_body, tuple(init))
    for c in range(n_chain):
        diag_accumulate(c, n_slots - 1)
    for c in range(n_chain):
        if 0 in blocks[c]:
            finalize(c, blocks[c].index(0), 0)


    def entry(c, field, g):
        return full_ref[c, field, g + 1]

    def scores(c, g):
        k0 = pl.multiple_of(entry(c, SCHED_KB, g) * tk, tk)
        q0 = pl.multiple_of(entry(c, SCHED_QB, g) * tk, tk)
        maxes = []
        for mp in range(2):
            s = jnp.dot(k_ref[mp, pl.ds(k0, tk), :], qt_ref[mp, :, pl.ds(q0, tk)],
                        preferred_element_type=F32)
            s_ref[c, mp] = s
            maxes.append(jnp.max(s, axis=0, keepdims=True))
        return tuple(maxes)

    def numerators(c, g, ms, maxes):
        off = slope2 * ((entry(c, SCHED_KB, g) - entry(c, SCHED_QB, g)) * tk).astype(F32)
        resume = entry(c, SCHED_FIRST, g) == 1
        slot = entry(c, SCHED_SLOT, g)
        new_ms, scales = [], []
        for mp in range(2):
            m_old = jnp.where(resume, m_ref[c, slot, mp], ms[mp])
            m_new = jnp.maximum(m_old, maxes[mp] + off)
            scales.append(jnp.exp2(m_old - m_new))
            p_ref[c, mp] = jnp.exp2(s_ref[c, mp] - (m_new - off)).astype(BF16)
            new_ms.append(m_new)
        return tuple(new_ms), tuple(scales)

    def accumulate(c, g, scales):
        k0 = pl.multiple_of(entry(c, SCHED_KB, g) * tk, tk)
        slot = entry(c, SCHED_SLOT, g)
        vt = vt_ref[:, pl.ds(k0, tk)]
        for mp in range(2):
            acc_ref[c, slot, mp] = scales[mp] * acc_ref[c, slot, mp] + jnp.dot(
                vt, p_ref[c, mp], preferred_element_type=F32)

    def finalize_step(c, g):
        finalize(c, entry(c, SCHED_SLOT, g), entry(c, SCHED_QB, g))

    def body(g, carry):
        out = []
        for c in range(n_chain):
            ms, scales, maxes = carry[c]
            accumulate(c, g - 1, scales)
            ms, scales = numerators(c, g, ms, maxes)
            maxes = scores(c, g + 1)
            out.append((ms, scales, maxes))
        for c in range(n_chain):
            pl.when(entry(c, SCHED_LAST, g - 1) == 1)(functools.partial(finalize_step, c, g - 1))
        return tuple(out)

    zero = jnp.zeros((1, tk), F32)
    init = []
    for c in range(n_chain):
        ms, scales = numerators(c, 0, (zero, zero), scores(c, 0))
        init.append((ms, scales, scores(c, 1)))
    state = lax.fori_loop(1, n_full, body, tuple(init))
    for c in range(n_chain):
        accumulate(c, n_full - 1, state[c][1])
        finalize_step(c, n_full - 1)


def _attention(qt, k, vt, slopes, lamp, subg, lam_init):
    bsz, _, _, seq = qt.shape
    n = bsz * seq
    tk = ATTN_K
    vrows = V_DIM + ONES_ROWS
    blocks, diag, full = _chain_schedule(seq // tk, N_CHAIN)
    n_slots = len(blocks[0])
    n_full = full.shape[2] - 2
    return pl.pallas_call(
        functools.partial(_attn_kernel, lam_init, blocks, n_full),
        grid=(bsz, N_HEADS),
        in_specs=[
            pl.BlockSpec(memory_space=pltpu.SMEM),
            pl.BlockSpec(memory_space=pltpu.SMEM),
            pl.BlockSpec(memory_space=pltpu.SMEM),
            pl.BlockSpec((None, 2, SLAB, seq), lambda b, h: (b, 0, h, 0)),
            pl.BlockSpec((2, seq, SLAB), lambda b, h: (0, b, h)),
            pl.BlockSpec((None, vrows, seq), lambda b, h: (b, h, 0)),
            pl.BlockSpec((4, HEAD_DIM), lambda b, h: (0, 0)),
            pl.BlockSpec((V_DIM, 1), lambda b, h: (0, 0)),
        ],
        out_specs=pl.BlockSpec((seq, V_DIM), lambda b, h: (b, h)),
        out_shape=jax.ShapeDtypeStruct((n, D_ATTN), BF16),
        scratch_shapes=[
            pltpu.VMEM((N_CHAIN, n_slots, 2, vrows, tk), F32),
            pltpu.VMEM((N_CHAIN, n_slots, 2, 1, tk), F32),
            pltpu.VMEM((N_CHAIN, 2, tk, tk), F32),
            pltpu.VMEM((N_CHAIN, 2, tk, tk), BF16),
        ],
        compiler_params=pltpu.CompilerParams(
            dimension_semantics=("arbitrary", "arbitrary"),
            vmem_limit_bytes=VMEM_LIMIT_BYTES),
        name="diff_attn",
    )(jnp.asarray(diag), jnp.asarray(full), slopes, qt, k, vt, lamp, subg)


def _merge_ffn_kernel(x_ref, o_ref, gp_ref, gb_ref, wpa_ref, wo_ref, g2_ref, b2_ref,
                      wg_ref, wu_ref, wd_ref, g3_ref, b3_ref, out_ref, h_ref):
    y_attn = jnp.dot(o_ref[...], wpa_ref[...], preferred_element_type=F32)
    mixed = (gp_ref[...] + gb_ref[...] * y_attn).astype(BF16)
    mix = jnp.dot(mixed, wo_ref[...], preferred_element_type=F32)
    x2 = _layer_norm(DEEPNORM_ALPHA * x_ref[...] + mix, g2_ref[...], b2_ref[...])
    out_ref[...] = _ffn_ln_block(x2, wg_ref, wu_ref, wd_ref, g3_ref, b3_ref, h_ref)


def _merge_ffn(x2d, o, gp, gb, wpa, wo, g2, b2, wg, wu, wd, g3, b3):
    n = x2d.shape[0]
    tm = MERGE_ROWS
    row_spec = pl.BlockSpec((tm, D_MODEL), lambda i: (i, 0))
    vec_spec = _resident((1, D_MODEL))
    return pl.pallas_call(
        _merge_ffn_kernel,
        grid=(n // tm,),
        in_specs=[row_spec, row_spec, row_spec, row_spec,
                  _resident((D_ATTN, D_MODEL)), _resident((D_MODEL, D_MODEL)), vec_spec, vec_spec,
                  _resident((D_MODEL, D_FF)), _resident((D_MODEL, D_FF)), _resident((D_FF, D_MODEL)),
                  vec_spec, vec_spec],
        out_specs=row_spec,
        out_shape=jax.ShapeDtypeStruct((n, D_MODEL), F32),
        scratch_shapes=[pltpu.VMEM((tm, D_FF), BF16)],
        compiler_params=pltpu.CompilerParams(
            dimension_semantics=("arbitrary",), vmem_limit_bytes=VMEM_LIMIT_BYTES),
        name="merge_ffn",
    )(x2d, o, gp, gb, wpa, wo, g2, b2, wg, wu, wd, g3, b3)


def _slope_pieces(slopes):
    c = slopes * LOG2E
    pieces = []
    for _ in range(N_SPLIT):
        piece = c.astype(BF16).astype(F32)
        pieces.append(piece)
        c = c - piece
    aux = jnp.zeros((N_HEADS, HEAD_DIM), F32)
    aux = aux.at[:, :2 * N_SPLIT].set(jnp.stack(pieces + pieces, axis=1))
    return aux.reshape(D_QK, 1)


def kernel(x, ffn1_w_gate, ffn1_w_up, ffn1_w_down, ln1_g, ln1_b, w_in, b_gate, pool_w, pool_b, pool_scale, lambda_q1, lambda_k1, lambda_q2, lambda_k2, subln_g, w_proj_pool, w_proj_attn, w_out, ln2_g, ln2_b, ffn2_w_gate, ffn2_w_up, ffn2_w_down, ln3_g, ln3_b):
    bsz, seq, d = x.shape
    n = bsz * seq
    h = x.reshape(n, d)
    slopes = jnp.exp2(-8.0 / N_HEADS * jnp.arange(1, N_HEADS + 1, dtype=F32))
    qaux = _slope_pieces(slopes)
    o_q, o_k, o_v, o_g = D_POOL, D_POOL + 2 * D_QK, D_POOL + 4 * D_QK, D_POOL + 4 * D_QK + D_ATTN
    for l in range(DEPTH):
        lam_init = 0.8 - 0.6 * math.exp(-0.3 * l)
        h = _ffn_ln(h, ffn1_w_gate[l].astype(BF16), ffn1_w_up[l].astype(BF16),
                    ffn1_w_down[l].astype(BF16), ln1_g[l][None], ln1_b[l][None])

        wi = w_in[l]
        qt, k, vt, gp, gb = _in_proj(
            h, seq,
            wi[:, :o_q].astype(BF16),
            wi[:, o_q:o_k].T.astype(BF16),
            wi[:, o_k:o_v].astype(BF16),
            wi[:, o_v:o_g].T.astype(BF16),
            wi[:, o_g:].astype(BF16),
            b_gate[l][None],
            qaux,
            pool_w[l].astype(BF16),
            pool_b[l][:, None, :],
            pool_scale[l][None],
            w_proj_pool[l].astype(BF16))

        lamp = jnp.stack([lambda_q1[l], lambda_k1[l], lambda_q2[l], lambda_k2[l]]).astype(F32)
        o = _attention(qt, k, vt, slopes, lamp, subln_g[l][:, None].astype(F32), lam_init)

        h = _merge_ffn(h, o, gp, gb, w_proj_attn[l].astype(BF16), w_out[l].astype(BF16),
                       ln2_g[l][None], ln2_b[l][None],
                       ffn2_w_gate[l].astype(BF16), ffn2_w_up[l].astype(BF16),
                       ffn2_w_down[l].astype(BF16), ln3_g[l][None], ln3_b[l][None])
    return h.reshape(bsz, seq, d)
```

```python
import functools
import math

import jax
import jax.numpy as jnp
import numpy as np
from jax import lax
from jax.experimental import pallas as pl
from jax.experimental.pallas import tpu as pltpu

F32 = jnp.float32
BF16 = jnp.bfloat16

D_MODEL = 1024
D_FF = 2816
POOL_WINDOWS = (2, 4, 8, 16)
POOL_GROUP_DIM = 128
D_POOL = len(POOL_WINDOWS) * POOL_GROUP_DIM
N_HEADS = 8
HEAD_DIM = 64
V_DIM = 2 * HEAD_DIM
D_ATTN = N_HEADS * V_DIM
D_QK = N_HEADS * HEAD_DIM
LN_EPS = 1e-5
RMS_EPS = 1e-5
DEPTH = 1
DEEPNORM_ALPHA = (2.0 * DEPTH) ** 0.25
LOG2E = math.log2(math.e)
SUBLANES = 8
POOL_HISTORY = SUBLANES * (max(POOL_WINDOWS).bit_length() - 1)
ONES_ROWS = 16
SLAB = 2 * HEAD_DIM
BF16_EXACT_INT = 256
N_SPLIT = 3

VMEM_LIMIT_BYTES = 56 * 1024 * 1024

FFN_ROWS = 512
MERGE_ROWS = 512
FFN_CHUNK = 256
ATTN_K = 512
N_CHAIN = 2


def _resident(shape):
    nd = len(shape)
    return pl.BlockSpec(shape, lambda *_: (0,) * nd, pipeline_mode=pl.Buffered(1))


def _layer_norm(r, g, b):
    mu = jnp.mean(r, axis=-1, keepdims=True)
    c = r - mu
    var = jnp.mean(c * c, axis=-1, keepdims=True)
    return c * lax.rsqrt(var + LN_EPS) * g + b


def _ffn_ln_block(x, wg_ref, wu_ref, wd_ref, g_ref, b_ref, h_ref):
    xb = x.astype(BF16)
    for c in range(0, D_FF, FFN_CHUNK):
        gate = jnp.dot(xb, wg_ref[:, c:c + FFN_CHUNK], preferred_element_type=F32)
        up = jnp.dot(xb, wu_ref[:, c:c + FFN_CHUNK], preferred_element_type=F32)
        h_ref[:, c:c + FFN_CHUNK] = (gate * jax.nn.sigmoid(gate) * up).astype(BF16)
    y = jnp.dot(h_ref[...], wd_ref[...], preferred_element_type=F32)
    return _layer_norm(DEEPNORM_ALPHA * x + 0.5 * y, g_ref[...], b_ref[...])


def _ffn_ln_kernel(x_ref, wg_ref, wu_ref, wd_ref, g_ref, b_ref, o_ref, h_ref):
    o_ref[...] = _ffn_ln_block(x_ref[...], wg_ref, wu_ref, wd_ref, g_ref, b_ref, h_ref)


def _ffn_ln(x2d, wg, wu, wd, g, b):
    n = x2d.shape[0]
    tm = FFN_ROWS
    return pl.pallas_call(
        _ffn_ln_kernel,
        grid=(n // tm,),
        in_specs=[
            pl.BlockSpec((tm, D_MODEL), lambda i: (i, 0)),
            _resident((D_MODEL, D_FF)),
            _resident((D_MODEL, D_FF)),
            _resident((D_FF, D_MODEL)),
            _resident((1, D_MODEL)),
            _resident((1, D_MODEL)),
        ],
        out_specs=pl.BlockSpec((tm, D_MODEL), lambda i: (i, 0)),
        out_shape=jax.ShapeDtypeStruct((n, D_MODEL), F32),
        scratch_shapes=[pltpu.VMEM((tm, D_FF), BF16)],
        compiler_params=pltpu.CompilerParams(
            dimension_semantics=("arbitrary",), vmem_limit_bytes=VMEM_LIMIT_BYTES),
        name="ffn_ln",
    )(x2d, wg, wu, wd, g, b)


def _trailing_window_sum(ubuf_ref, wbuf_ref, lanes, w, tm):
    end = POOL_HISTORY + tm
    read = lambda a, b: ubuf_ref[a:b, lanes]
    for k in range(1, w.bit_length()):
        span, start = w >> k, k * SUBLANES
        level = read(start, end) + read(start - span, end - span)
        if span == 1:
            return level[POOL_HISTORY - start:]
        dst = wbuf_ref.at[k % 2]
        dst[start:end, :] = level
        read = lambda a, b, dst=dst: dst[a:b, :]


def _in_proj_kernel(tiles_per_seq, x_ref, wpool_ref, wqt_ref, wk_ref, wvt_ref, wg_ref, bg_ref,
                    qaux_ref, poolw_ref, poolb_ref, pscale_ref, wpp_ref,
                    qt_ref, k_ref, vt_ref, gp_ref, gb_ref, ubuf_ref, wbuf_ref):
    i = pl.program_id(0)
    tm = x_ref.shape[0]
    xb = x_ref[...].astype(BF16)
    seq_tile = i % tiles_per_seq
    nt = (((1,), (1,)), ((), ()))

    qt = lax.dot_general(wqt_ref[...], xb, nt, preferred_element_type=F32)
    qt = (qt * (HEAD_DIM ** -0.5 * LOG2E)).astype(BF16)
    for hd in range(N_HEADS):
        aux = jnp.broadcast_to(qaux_ref[hd * HEAD_DIM:(hd + 1) * HEAD_DIM, :],
                               (HEAD_DIM, tm)).astype(BF16)
        feat_rows = hd * SLAB + (hd % 2) * HEAD_DIM
        aux_rows = hd * SLAB + (1 - hd % 2) * HEAD_DIM
        for mp in range(2):
            src = mp * D_QK + hd * HEAD_DIM
            qt_ref[mp, feat_rows:feat_rows + HEAD_DIM, :] = qt[src:src + HEAD_DIM]
            qt_ref[mp, aux_rows:aux_rows + HEAD_DIM, :] = aux

    k = jnp.dot(xb, wk_ref[...], preferred_element_type=F32)
    row = lax.broadcasted_iota(jnp.int32, (tm, SLAB), 0)
    lane = lax.broadcasted_iota(jnp.int32, (tm, SLAB), 1)
    j = (seq_tile * tm + row) & (ATTN_K - 1)
    jlo = (j & (BF16_EXACT_INT - 1)).astype(F32)
    jhi = (j - (j & (BF16_EXACT_INT - 1))).astype(F32)
    low = lane < HEAD_DIM
    al = lane & (HEAD_DIM - 1)
    aux = jnp.where(al < N_SPLIT, jlo, jnp.where(al < 2 * N_SPLIT, jhi, 0.0))
    for mp in range(2):
        for pair in range(N_HEADS // 2):
            src = mp * D_QK + pair * SLAB
            feats = k[:, src:src + SLAB]
            dst = 2 * pair * SLAB
            k_ref[mp, :, dst:dst + SLAB] = jnp.where(low, feats, aux).astype(BF16)
            k_ref[mp, :, dst + SLAB:dst + 2 * SLAB] = jnp.where(low, aux, feats).astype(BF16)

    vt = lax.dot_general(wvt_ref[...], xb, nt, preferred_element_type=F32)
    vrows = V_DIM + ONES_ROWS
    for hd in range(N_HEADS):
        vt_ref[hd * vrows:hd * vrows + V_DIM, :] = vt[hd * V_DIM:(hd + 1) * V_DIM].astype(BF16)
        vt_ref[hd * vrows + V_DIM:(hd + 1) * vrows, :] = jnp.ones((ONES_ROWS, tm), BF16)

    gates = jax.nn.sigmoid(jnp.dot(xb, wg_ref[...], preferred_element_type=F32) + bg_ref[...])
    gb_ref[...] = gates[:, D_MODEL:]

    hist = POOL_HISTORY
    u = jnp.dot(xb, wpool_ref[...], preferred_element_type=F32)

    @pl.when(seq_tile == 0)
    def _():
        ubuf_ref[0:hist, :] = jnp.zeros((hist, D_POOL), F32)

    @pl.when(seq_tile != 0)
    def _():
        ubuf_ref[0:hist, :] = ubuf_ref[tm:tm + hist, :]

    ubuf_ref[hist:hist + tm, :] = u

    pos = seq_tile * tm + lax.broadcasted_iota(jnp.int32, (tm, 1), 0)
    zs = []
    for g, w in enumerate(POOL_WINDOWS):
        lanes = slice(g * POOL_GROUP_DIM, (g + 1) * POOL_GROUP_DIM)
        ug = u[:, lanes]
        win = _trailing_window_sum(ubuf_ref, wbuf_ref, lanes, w, tm)
        cnt = jnp.minimum(pos + 1, w).astype(F32)
        z = (win / cnt - ug).astype(BF16)
        zg = jnp.dot(z, poolw_ref[g], preferred_element_type=F32) + poolb_ref[g]
        zs.append(zg)
    zall = (jnp.concatenate(zs, axis=-1) * pscale_ref[...]).astype(BF16)
    y_pool = jnp.dot(zall, wpp_ref[...], preferred_element_type=F32)
    gp_ref[...] = gates[:, :D_MODEL] * y_pool


def _in_proj(x2d, seq, wpool, wqt, wk, wvt, wg, bg, qaux, poolw, poolb, pscale, wpp):
    n = x2d.shape[0]
    bsz = n // seq
    tm = FFN_ROWS
    tps = seq // tm
    row_spec = pl.BlockSpec((tm, D_MODEL), lambda i: (i, 0))
    vt_rows = N_HEADS * (V_DIM + ONES_ROWS)
    return pl.pallas_call(
        functools.partial(_in_proj_kernel, tps),
        grid=(n // tm,),
        in_specs=[
            row_spec,
            _resident((D_MODEL, D_POOL)),
            _resident((2 * D_QK, D_MODEL)),
            _resident((D_MODEL, 2 * D_QK)),
            _resident((D_ATTN, D_MODEL)),
            _resident((D_MODEL, 2 * D_MODEL)),
            _resident((1, 2 * D_MODEL)),
            _resident((D_QK, 1)),
            _resident((len(POOL_WINDOWS), POOL_GROUP_DIM, POOL_GROUP_DIM)),
            _resident((len(POOL_WINDOWS), 1, POOL_GROUP_DIM)),
            _resident((1, D_POOL)),
            _resident((D_POOL, D_MODEL)),
        ],
        out_specs=[
            pl.BlockSpec((None, 2, N_HEADS * SLAB, tm), lambda i: (i // tps, 0, 0, i % tps)),
            pl.BlockSpec((2, tm, N_HEADS * SLAB), lambda i: (0, i, 0)),
            pl.BlockSpec((None, vt_rows, tm), lambda i: (i // tps, 0, i % tps)),
            row_spec,
            row_spec,
        ],
        out_shape=[
            jax.ShapeDtypeStruct((bsz, 2, N_HEADS * SLAB, seq), BF16),
            jax.ShapeDtypeStruct((2, n, N_HEADS * SLAB), BF16),
            jax.ShapeDtypeStruct((bsz, vt_rows, seq), BF16),
            jax.ShapeDtypeStruct((n, D_MODEL), F32),
            jax.ShapeDtypeStruct((n, D_MODEL), F32),
        ],
        scratch_shapes=[pltpu.VMEM((POOL_HISTORY + tm, D_POOL), F32),
                        pltpu.VMEM((2, POOL_HISTORY + tm, POOL_GROUP_DIM), F32)],
        compiler_params=pltpu.CompilerParams(
            dimension_semantics=("arbitrary",), vmem_limit_bytes=VMEM_LIMIT_BYTES),
        name="in_proj",
    )(x2d, wpool, wqt, wk, wvt, wg, bg, qaux, poolw, poolb, pscale, wpp)


SCHED_KB, SCHED_QB, SCHED_SLOT, SCHED_FIRST = range(4)
N_SCHED_FIELDS = 4


def _chain_schedule(nq, n_chain):
    loads = [0] * n_chain
    blocks = [[] for _ in range(n_chain)]
    for qb in reversed(range(nq)):
        c = loads.index(min(loads))
        blocks[c].append(qb)
        loads[c] += qb + 1
    assert len(set(loads)) == 1 and len({len(b) for b in blocks}) == 1, (
        f"query blocks do not split evenly over {n_chain} chains: {blocks}")
    n_full = loads[0] - len(blocks[0])
    diag = np.zeros((n_chain, len(blocks[0]) + 2), np.int32)
    full = np.zeros((n_chain, N_SCHED_FIELDS, n_full + 2), np.int32)
    for c in range(n_chain):
        diag[c] = [blocks[c][0]] + blocks[c] + [blocks[c][-1]]
        steps = []
        for slot, qb in enumerate(blocks[c]):
            steps += [(kb, qb, slot, int(kb == 0)) for kb in range(qb)]
        steps = [steps[0][:3] + (0,)] + steps + [steps[-1][:3] + (0,)]
        full[c] = np.array(steps, np.int32).T
    return blocks, diag, full


def _attn_kernel(lam_init, blocks, n_full, diag_ref, full_ref, slopes_ref, qt_ref, k_ref, vt_ref,
                 lamp_ref, subg_ref, o_ref, acc_ref, m_ref, s_ref, p_ref):
    h = pl.program_id(1)
    tk = ATTN_K
    n_chain = len(blocks)
    n_slots = len(blocks[0])
    slope2 = slopes_ref[h] * LOG2E

    lp = lamp_ref[...]
    lam = (jnp.exp(jnp.sum(lp[0:1] * lp[1:2], axis=-1, keepdims=True))
           - jnp.exp(jnp.sum(lp[2:3] * lp[3:4], axis=-1, keepdims=True)) + lam_init)

    def finalize(c, slot, qb):
        q0 = pl.multiple_of(qb * tk, tk)
        acc1 = acc_ref[c, slot, 0]
        acc2 = acc_ref[c, slot, 1]
        o = acc1[:V_DIM] / acc1[V_DIM:V_DIM + 1] - lam * (acc2[:V_DIM] / acc2[V_DIM:V_DIM + 1])
        o = o * lax.rsqrt(jnp.mean(o * o, axis=0, keepdims=True) + RMS_EPS)
        o = o * subg_ref[...] * (1.0 - lam_init)
        o_ref[pl.ds(q0, tk), :] = o.T.astype(o_ref.dtype)


    key = lax.broadcasted_iota(jnp.int32, (tk, tk), 0)
    qry = lax.broadcasted_iota(jnp.int32, (tk, tk), 1)

    def diag_block(c, j):
        return pl.multiple_of(diag_ref[c, j + 1] * tk, tk)

    def diag_scores(c, j):
        b0 = diag_block(c, j)
        maxes = []
        for mp in range(2):
            s = jnp.dot(k_ref[mp, pl.ds(b0, tk), :], qt_ref[mp, :, pl.ds(b0, tk)],
                        preferred_element_type=F32)
            s = jnp.where(key <= qry, s, -jnp.inf)
            s_ref[c, mp] = s
            maxes.append(jnp.max(s, axis=0, keepdims=True))
        return tuple(maxes)

    def diag_numerators(c, j, maxes):
        for mp in range(2):
            m_ref[c, j, mp] = maxes[mp]
            p_ref[c, mp] = jnp.exp2(s_ref[c, mp] - maxes[mp]).astype(BF16)

    def diag_accumulate(c, j):
        vt = vt_ref[:, pl.ds(diag_block(c, j), tk)]
        for mp in range(2):
            acc_ref[c, j, mp] = jnp.dot(vt, p_ref[c, mp], preferred_element_type=F32)

    def diag_body(j, carry):
        out = []
        for c in range(n_chain):
            diag_accumulate(c, j - 1)
            diag_numerators(c, j, carry[c])
            out.append(diag_scores(c, j + 1))
        return tuple(out)

    init = []
    for c in range(n_chain):
        diag_numerators(c, 0, diag_scores(c, 0))
        init.append(diag_scores(c, 1))
    lax.fori_loop(1, n_slots, diag_body, tuple(init))
    for c in range(n_chain):
        diag_accumulate(c, n_slots - 1)

    def entry(c, field, g):
        return full_ref[c, field, g + 1]

    def scores(c, g):
        k0 = pl.multiple_of(entry(c, SCHED_KB, g) * tk, tk)
        q0 = pl.multiple_of(entry(c, SCHED_QB, g) * tk, tk)
        maxes = []
        for mp in range(2):
            s = jnp.dot(k_ref[mp, pl.ds(k0, tk), :], qt_ref[mp, :, pl.ds(q0, tk)],
                        preferred_element_type=F32)
            s_ref[c, mp] = s
            maxes.append(jnp.max(s, axis=0, keepdims=True))
        return tuple(maxes)

    def numerators(c, g, ms, maxes):
        off = slope2 * ((entry(c, SCHED_KB, g) - entry(c, SCHED_QB, g)) * tk).astype(F32)
        resume = entry(c, SCHED_FIRST, g) == 1
        slot = entry(c, SCHED_SLOT, g)
        new_ms, scales = [], []
        for mp in range(2):
            m_old = jnp.where(resume, m_ref[c, slot, mp], ms[mp])
            m_new = jnp.maximum(m_old, maxes[mp] + off)
            scales.append(jnp.exp2(m_old - m_new))
            p_ref[c, mp] = jnp.exp2(s_ref[c, mp] - (m_new - off)).astype(BF16)
            new_ms.append(m_new)
        return tuple(new_ms), tuple(scales)

    def accumulate(c, g, scales):
        k0 = pl.multiple_of(entry(c, SCHED_KB, g) * tk, tk)
        slot = entry(c, SCHED_SLOT, g)
        vt = vt_ref[:, pl.ds(k0, tk)]
        for mp in range(2):
            acc_ref[c, slot, mp] = scales[mp] * acc_ref[c, slot, mp] + jnp.dot(
                vt, p_ref[c, mp], preferred_element_type=F32)

    def body(g, carry):
        out = []
        for c in range(n_chain):
            ms, scales, maxes = carry[c]
            accumulate(c, g - 1, scales)
            ms, scales = numerators(c, g, ms, maxes)
            maxes = scores(c, g + 1)
            out.append((ms, scales, maxes))
        return tuple(out)

    zero = jnp.zeros((1, tk), F32)
    init = []
    for c in range(n_chain):
        ms, scales = numerators(c, 0, (zero, zero), scores(c, 0))
        init.append((ms, scales, scores(c, 1)))
    state = lax.fori_loop(1, n_full, body, tuple(init))
    for c in range(n_chain):
        accumulate(c, n_full - 1, state[c][1])

    def finalize_slot(slot, carry):
        for c in range(n_chain):
            finalize(c, slot, diag_ref[c, slot + 1])
        return carry

    lax.fori_loop(0, n_slots, finalize_slot, 0)


def _attention(qt, k, vt, slopes, lamp, subg, lam_init):
    bsz, _, _, seq = qt.shape
    n = bsz * seq
    tk = ATTN_K
    vrows = V_DIM + ONES_ROWS
    blocks, diag, full = _chain_schedule(seq // tk, N_CHAIN)
    n_slots = len(blocks[0])
    n_full = full.shape[2] - 2
    return pl.pallas_call(
        functools.partial(_attn_kernel, lam_init, blocks, n_full),
        grid=(bsz, N_HEADS),
        in_specs=[
            pl.BlockSpec(memory_space=pltpu.SMEM),
            pl.BlockSpec(memory_space=pltpu.SMEM),
            pl.BlockSpec(memory_space=pltpu.SMEM),
            pl.BlockSpec((None, 2, SLAB, seq), lambda b, h: (b, 0, h, 0)),
            pl.BlockSpec((2, seq, SLAB), lambda b, h: (0, b, h)),
            pl.BlockSpec((None, vrows, seq), lambda b, h: (b, h, 0)),
            pl.BlockSpec((4, HEAD_DIM), lambda b, h: (0, 0)),
            pl.BlockSpec((V_DIM, 1), lambda b, h: (0, 0)),
        ],
        out_specs=pl.BlockSpec((seq, V_DIM), lambda b, h: (b, h)),
        out_shape=jax.ShapeDtypeStruct((n, D_ATTN), BF16),
        scratch_shapes=[
            pltpu.VMEM((N_CHAIN, n_slots, 2, vrows, tk), F32),
            pltpu.VMEM((N_CHAIN, n_slots, 2, 1, tk), F32),
            pltpu.VMEM((N_CHAIN, 2, tk, tk), F32),
            pltpu.VMEM((N_CHAIN, 2, tk, tk), BF16),
        ],
        compiler_params=pltpu.CompilerParams(
            dimension_semantics=("arbitrary", "arbitrary"),
            vmem_limit_bytes=VMEM_LIMIT_BYTES),
        name="diff_attn",
    )(jnp.asarray(diag), jnp.asarray(full), slopes, qt, k, vt, lamp, subg)


def _merge_ffn_kernel(x_ref, o_ref, gp_ref, gb_ref, wpa_ref, wo_ref, g2_ref, b2_ref,
                      wg_ref, wu_ref, wd_ref, g3_ref, b3_ref, out_ref, h_ref):
    y_attn = jnp.dot(o_ref[...], wpa_ref[...], preferred_element_type=F32)
    mixed = (gp_ref[...] + gb_ref[...] * y_attn).astype(BF16)
    mix = jnp.dot(mixed, wo_ref[...], preferred_element_type=F32)
    x2 = _layer_norm(DEEPNORM_ALPHA * x_ref[...] + mix, g2_ref[...], b2_ref[...])
    out_ref[...] = _ffn_ln_block(x2, wg_ref, wu_ref, wd_ref, g3_ref, b3_ref, h_ref)


def _merge_ffn(x2d, o, gp, gb, wpa, wo, g2, b2, wg, wu, wd, g3, b3):
    n = x2d.shape[0]
    tm = MERGE_ROWS
    row_spec = pl.BlockSpec((tm, D_MODEL), lambda i: (i, 0))
    vec_spec = _resident((1, D_MODEL))
    return pl.pallas_call(
        _merge_ffn_kernel,
        grid=(n // tm,),
        in_specs=[row_spec, row_spec, row_spec, row_spec,
                  _resident((D_ATTN, D_MODEL)), _resident((D_MODEL, D_MODEL)), vec_spec, vec_spec,
                  _resident((D_MODEL, D_FF)), _resident((D_MODEL, D_FF)), _resident((D_FF, D_MODEL)),
                  vec_spec, vec_spec],
        out_specs=row_spec,
        out_shape=jax.ShapeDtypeStruct((n, D_MODEL), F32),
        scratch_shapes=[pltpu.VMEM((tm, D_FF), BF16)],
        compiler_params=pltpu.CompilerParams(
            dimension_semantics=("arbitrary",), vmem_limit_bytes=VMEM_LIMIT_BYTES),
        name="merge_ffn",
    )(x2d, o, gp, gb, wpa, wo, g2, b2, wg, wu, wd, g3, b3)


def _slope_pieces(slopes):
    c = slopes * LOG2E
    pieces = []
    for _ in range(N_SPLIT):
        piece = c.astype(BF16).astype(F32)
        pieces.append(piece)
        c = c - piece
    aux = jnp.zeros((N_HEADS, HEAD_DIM), F32)
    aux = aux.at[:, :2 * N_SPLIT].set(jnp.stack(pieces + pieces, axis=1))
    return aux.reshape(D_QK, 1)


def kernel(x, ffn1_w_gate, ffn1_w_up, ffn1_w_down, ln1_g, ln1_b, w_in, b_gate, pool_w, pool_b, pool_scale, lambda_q1, lambda_k1, lambda_q2, lambda_k2, subln_g, w_proj_pool, w_proj_attn, w_out, ln2_g, ln2_b, ffn2_w_gate, ffn2_w_up, ffn2_w_down, ln3_g, ln3_b):
    bsz, seq, d = x.shape
    n = bsz * seq
    h = x.reshape(n, d)
    slopes = jnp.exp2(-8.0 / N_HEADS * jnp.arange(1, N_HEADS + 1, dtype=F32))
    qaux = _slope_pieces(slopes)
    o_q, o_k, o_v, o_g = D_POOL, D_POOL + 2 * D_QK, D_POOL + 4 * D_QK, D_POOL + 4 * D_QK + D_ATTN
    for l in range(DEPTH):
        lam_init = 0.8 - 0.6 * math.exp(-0.3 * l)
        h = _ffn_ln(h, ffn1_w_gate[l].astype(BF16), ffn1_w_up[l].astype(BF16),
                    ffn1_w_down[l].astype(BF16), ln1_g[l][None], ln1_b[l][None])

        wi = w_in[l]
        qt, k, vt, gp, gb = _in_proj(
            h, seq,
            wi[:, :o_q].astype(BF16),
            wi[:, o_q:o_k].T.astype(BF16),
            wi[:, o_k:o_v].astype(BF16),
            wi[:, o_v:o_g].T.astype(BF16),
            wi[:, o_g:].astype(BF16),
            b_gate[l][None],
            qaux,
            pool_w[l].astype(BF16),
            pool_b[l][:, None, :],
            pool_scale[l][None],
            w_proj_pool[l].astype(BF16))

        lamp = jnp.stack([lambda_q1[l], lambda_k1[l], lambda_q2[l], lambda_k2[l]]).astype(F32)
        o = _attention(qt, k, vt, slopes, lamp, subln_g[l][:, None].astype(F32), lam_init)

        h = _merge_ffn(h, o, gp, gb, w_proj_attn[l].astype(BF16), w_out[l].astype(BF16),
                       ln2_g[l][None], ln2_b[l][None],
                       ffn2_w_gate[l].astype(BF16), ffn2_w_up[l].astype(BF16),
                       ffn2_w_down[l].astype(BF16), ln3_g[l][None], ln3_b[l][None])
    return h.reshape(bsz, seq, d)
```

```python
import functools
import math

import jax
import jax.numpy as jnp
import numpy as np
from jax import lax
from jax.experimental import pallas as pl
from jax.experimental.pallas import tpu as pltpu

F32 = jnp.float32
BF16 = jnp.bfloat16

D_MODEL = 1024
D_FF = 2816
POOL_WINDOWS = (2, 4, 8, 16)
POOL_GROUP_DIM = 128
D_POOL = len(POOL_WINDOWS) * POOL_GROUP_DIM
N_HEADS = 8
HEAD_DIM = 64
V_DIM = 2 * HEAD_DIM
D_ATTN = N_HEADS * V_DIM
D_QK = N_HEADS * HEAD_DIM
LN_EPS = 1e-5
RMS_EPS = 1e-5
DEPTH = 1
DEEPNORM_ALPHA = (2.0 * DEPTH) ** 0.25
LOG2E = math.log2(math.e)
SUBLANES = 8
POOL_HISTORY = SUBLANES * (max(POOL_WINDOWS).bit_length() - 1)
ONES_ROWS = 16
SLAB = 2 * HEAD_DIM
BF16_EXACT_INT = 256
N_SPLIT = 3

VMEM_LIMIT_BYTES = 56 * 1024 * 1024

FFN_ROWS = 512
MERGE_ROWS = 512
FFN_CHUNK = 256
ATTN_K = 512
N_CHAIN = 2


def _resident(shape):
    nd = len(shape)
    return pl.BlockSpec(shape, lambda *_: (0,) * nd, pipeline_mode=pl.Buffered(1))


def _sigmoid(x):
    return 0.5 * jnp.tanh(0.5 * x) + 0.5


def _layer_norm(r, g, b):
    mu = jnp.mean(r, axis=-1, keepdims=True)
    c = r - mu
    var = jnp.mean(c * c, axis=-1, keepdims=True)
    return c * lax.rsqrt(var + LN_EPS) * g + b


def _ffn_ln_block(x, wg_ref, wu_ref, wd_ref, g_ref, b_ref, h_ref):
    xb = x.astype(BF16)
    for c in range(0, D_FF, FFN_CHUNK):
        gate = jnp.dot(xb, wg_ref[:, c:c + FFN_CHUNK], preferred_element_type=F32)
        up = jnp.dot(xb, wu_ref[:, c:c + FFN_CHUNK], preferred_element_type=F32)
        h_ref[:, c:c + FFN_CHUNK] = (gate * _sigmoid(gate) * up).astype(BF16)
    y = jnp.dot(h_ref[...], wd_ref[...], preferred_element_type=F32)
    return _layer_norm(DEEPNORM_ALPHA * x + 0.5 * y, g_ref[...], b_ref[...])


def _ffn_ln_kernel(x_ref, wg_ref, wu_ref, wd_ref, g_ref, b_ref, o_ref, h_ref):
    o_ref[...] = _ffn_ln_block(x_ref[...], wg_ref, wu_ref, wd_ref, g_ref, b_ref, h_ref)


def _ffn_ln(x2d, wg, wu, wd, g, b):
    n = x2d.shape[0]
    tm = FFN_ROWS
    return pl.pallas_call(
        _ffn_ln_kernel,
        grid=(n // tm,),
        in_specs=[
            pl.BlockSpec((tm, D_MODEL), lambda i: (i, 0)),
            _resident((D_MODEL, D_FF)),
            _resident((D_MODEL, D_FF)),
            _resident((D_FF, D_MODEL)),
            _resident((1, D_MODEL)),
            _resident((1, D_MODEL)),
        ],
        out_specs=pl.BlockSpec((tm, D_MODEL), lambda i: (i, 0)),
        out_shape=jax.ShapeDtypeStruct((n, D_MODEL), F32),
        scratch_shapes=[pltpu.VMEM((tm, D_FF), BF16)],
        compiler_params=pltpu.CompilerParams(
            dimension_semantics=("arbitrary",), vmem_limit_bytes=VMEM_LIMIT_BYTES),
        name="ffn_ln",
    )(x2d, wg, wu, wd, g, b)


def _trailing_window_sum(ubuf_ref, wbuf_ref, lanes, w, tm):
    end = POOL_HISTORY + tm
    read = lambda a, b: ubuf_ref[a:b, lanes]
    for k in range(1, w.bit_length()):
        span, start = w >> k, k * SUBLANES
        level = read(start, end) + read(start - span, end - span)
        if span == 1:
            return level[POOL_HISTORY - start:]
        dst = wbuf_ref.at[k % 2]
        dst[start:end, :] = level
        read = lambda a, b, dst=dst: dst[a:b, :]


def _in_proj_kernel(tiles_per_seq, x_ref, wpool_ref, wqt_ref, wk_ref, wvt_ref, wg_ref, bg_ref,
                    qaux_ref, poolw_ref, poolb_ref, pscale_ref, wpp_ref,
                    qt_ref, k_ref, vt_ref, gp_ref, gb_ref, ubuf_ref, wbuf_ref):
    i = pl.program_id(0)
    tm = x_ref.shape[0]
    xb = x_ref[...].astype(BF16)
    seq_tile = i % tiles_per_seq
    nt = (((1,), (1,)), ((), ()))

    qt = lax.dot_general(wqt_ref[...], xb, nt, preferred_element_type=F32)
    qt = (qt * (HEAD_DIM ** -0.5 * LOG2E)).astype(BF16)
    for hd in range(N_HEADS):
        aux = jnp.broadcast_to(qaux_ref[hd * HEAD_DIM:(hd + 1) * HEAD_DIM, :],
                               (HEAD_DIM, tm)).astype(BF16)
        feat_rows = hd * SLAB + (hd % 2) * HEAD_DIM
        aux_rows = hd * SLAB + (1 - hd % 2) * HEAD_DIM
        for mp in range(2):
            src = mp * D_QK + hd * HEAD_DIM
            qt_ref[mp, feat_rows:feat_rows + HEAD_DIM, :] = qt[src:src + HEAD_DIM]
            qt_ref[mp, aux_rows:aux_rows + HEAD_DIM, :] = aux

    k = jnp.dot(xb, wk_ref[...], preferred_element_type=F32)
    row = lax.broadcasted_iota(jnp.int32, (tm, SLAB), 0)
    lane = lax.broadcasted_iota(jnp.int32, (tm, SLAB), 1)
    j = (seq_tile * tm + row) & (ATTN_K - 1)
    jlo = (j & (BF16_EXACT_INT - 1)).astype(F32)
    jhi = (j - (j & (BF16_EXACT_INT - 1))).astype(F32)
    low = lane < HEAD_DIM
    al = lane & (HEAD_DIM - 1)
    aux = jnp.where(al < N_SPLIT, jlo, jnp.where(al < 2 * N_SPLIT, jhi, 0.0))
    for mp in range(2):
        for pair in range(N_HEADS // 2):
            src = mp * D_QK + pair * SLAB
            feats = k[:, src:src + SLAB]
            dst = 2 * pair * SLAB
            k_ref[mp, :, dst:dst + SLAB] = jnp.where(low, feats, aux).astype(BF16)
            k_ref[mp, :, dst + SLAB:dst + 2 * SLAB] = jnp.where(low, aux, feats).astype(BF16)

    vt = lax.dot_general(wvt_ref[...], xb, nt, preferred_element_type=F32)
    vrows = V_DIM + ONES_ROWS
    for hd in range(N_HEADS):
        vt_ref[hd * vrows:hd * vrows + V_DIM, :] = vt[hd * V_DIM:(hd + 1) * V_DIM].astype(BF16)
        vt_ref[hd * vrows + V_DIM:(hd + 1) * vrows, :] = jnp.ones((ONES_ROWS, tm), BF16)

    gates = _sigmoid(jnp.dot(xb, wg_ref[...], preferred_element_type=F32) + bg_ref[...])
    gb_ref[...] = gates[:, D_MODEL:]

    hist = POOL_HISTORY
    u = jnp.dot(xb, wpool_ref[...], preferred_element_type=F32)

    @pl.when(seq_tile == 0)
    def _():
        ubuf_ref[0:hist, :] = jnp.zeros((hist, D_POOL), F32)

    @pl.when(seq_tile != 0)
    def _():
        ubuf_ref[0:hist, :] = ubuf_ref[tm:tm + hist, :]

    ubuf_ref[hist:hist + tm, :] = u

    pos = seq_tile * tm + lax.broadcasted_iota(jnp.int32, (tm, 1), 0)
    zs = []
    for g, w in enumerate(POOL_WINDOWS):
        lanes = slice(g * POOL_GROUP_DIM, (g + 1) * POOL_GROUP_DIM)
        ug = u[:, lanes]
        win = _trailing_window_sum(ubuf_ref, wbuf_ref, lanes, w, tm)
        cnt = jnp.minimum(pos + 1, w).astype(F32)
        z = (win / cnt - ug).astype(BF16)
        zg = jnp.dot(z, poolw_ref[g], preferred_element_type=F32) + poolb_ref[g]
        zs.append(zg)
    zall = (jnp.concatenate(zs, axis=-1) * pscale_ref[...]).astype(BF16)
    y_pool = jnp.dot(zall, wpp_ref[...], preferred_element_type=F32)
    gp_ref[...] = gates[:, :D_MODEL] * y_pool


def _in_proj(x2d, seq, wpool, wqt, wk, wvt, wg, bg, qaux, poolw, poolb, pscale, wpp):
    n = x2d.shape[0]
    bsz = n // seq
    tm = FFN_ROWS
    tps = seq // tm
    row_spec = pl.BlockSpec((tm, D_MODEL), lambda i: (i, 0))
    vt_rows = N_HEADS * (V_DIM + ONES_ROWS)
    return pl.pallas_call(
        functools.partial(_in_proj_kernel, tps),
        grid=(n // tm,),
        in_specs=[
            row_spec,
            _resident((D_MODEL, D_POOL)),
            _resident((2 * D_QK, D_MODEL)),
            _resident((D_MODEL, 2 * D_QK)),
            _resident((D_ATTN, D_MODEL)),
            _resident((D_MODEL, 2 * D_MODEL)),
            _resident((1, 2 * D_MODEL)),
            _resident((D_QK, 1)),
            _resident((len(POOL_WINDOWS), POOL_GROUP_DIM, POOL_GROUP_DIM)),
            _resident((len(POOL_WINDOWS), 1, POOL_GROUP_DIM)),
            _resident((1, D_POOL)),
            _resident((D_POOL, D_MODEL)),
        ],
        out_specs=[
            pl.BlockSpec((None, 2, N_HEADS * SLAB, tm), lambda i: (i // tps, 0, 0, i % tps)),
            pl.BlockSpec((2, tm, N_HEADS * SLAB), lambda i: (0, i, 0)),
            pl.BlockSpec((None, vt_rows, tm), lambda i: (i // tps, 0, i % tps)),
            row_spec,
            row_spec,
        ],
        out_shape=[
            jax.ShapeDtypeStruct((bsz, 2, N_HEADS * SLAB, seq), BF16),
            jax.ShapeDtypeStruct((2, n, N_HEADS * SLAB), BF16),
            jax.ShapeDtypeStruct((bsz, vt_rows, seq), BF16),
            jax.ShapeDtypeStruct((n, D_MODEL), F32),
            jax.ShapeDtypeStruct((n, D_MODEL), F32),
        ],
        scratch_shapes=[pltpu.VMEM((POOL_HISTORY + tm, D_POOL), F32),
                        pltpu.VMEM((2, POOL_HISTORY + tm, POOL_GROUP_DIM), F32)],
        compiler_params=pltpu.CompilerParams(
            dimension_semantics=("arbitrary",), vmem_limit_bytes=VMEM_LIMIT_BYTES),
        name="in_proj",
    )(x2d, wpool, wqt, wk, wvt, wg, bg, qaux, poolw, poolb, pscale, wpp)


SCHED_KB, SCHED_QB, SCHED_SLOT, SCHED_FIRST = range(4)
N_SCHED_FIELDS = 4


def _chain_schedule(nq, n_chain):
    loads = [0] * n_chain
    blocks = [[] for _ in range(n_chain)]
    for qb in reversed(range(nq)):
        c = loads.index(min(loads))
        blocks[c].append(qb)
        loads[c] += qb + 1
    assert len(set(loads)) == 1 and len({len(b) for b in blocks}) == 1, (
        f"query blocks do not split evenly over {n_chain} chains: {blocks}")
    n_full = loads[0] - len(blocks[0])
    diag = np.zeros((n_chain, len(blocks[0]) + 2), np.int32)
    full = np.zeros((n_chain, N_SCHED_FIELDS, n_full + 2), np.int32)
    for c in range(n_chain):
        diag[c] = [blocks[c][0]] + blocks[c] + [blocks[c][-1]]
        steps = []
        for slot, qb in enumerate(blocks[c]):
            steps += [(kb, qb, slot, int(kb == 0)) for kb in range(qb)]
        steps = [steps[0][:3] + (0,)] + steps + [steps[-1][:3] + (0,)]
        full[c] = np.array(steps, np.int32).T
    return blocks, diag, full


def _attn_kernel(lam_init, blocks, n_full, diag_ref, full_ref, slopes_ref, qt_ref, k_ref, vt_ref,
                 lamp_ref, subg_ref, o_ref, acc_ref, m_ref, s_ref, p_ref):
    h = pl.program_id(1)
    tk = ATTN_K
    n_chain = len(blocks)
    n_slots = len(blocks[0])
    slope2 = slopes_ref[h] * LOG2E

    lp = lamp_ref[...]
    lam = (jnp.exp(jnp.sum(lp[0:1] * lp[1:2], axis=-1, keepdims=True))
           - jnp.exp(jnp.sum(lp[2:3] * lp[3:4], axis=-1, keepdims=True)) + lam_init)

    def finalize(c, slot, qb):
        q0 = pl.multiple_of(qb * tk, tk)
        acc1 = acc_ref[c, slot, 0]
        acc2 = acc_ref[c, slot, 1]
        o = acc1[:V_DIM] / acc1[V_DIM:V_DIM + 1] - lam * (acc2[:V_DIM] / acc2[V_DIM:V_DIM + 1])
        o = o * lax.rsqrt(jnp.mean(o * o, axis=0, keepdims=True) + RMS_EPS)
        o = o * subg_ref[...] * (1.0 - lam_init)
        o_ref[pl.ds(q0, tk), :] = o.T.astype(o_ref.dtype)


    half = tk // 2
    causal = (lax.broadcasted_iota(jnp.int32, (half, half), 0)
              <= lax.broadcasted_iota(jnp.int32, (half, half), 1))

    def diag_block(c, j):
        return pl.multiple_of(diag_ref[c, j + 1] * tk, tk)

    def diag_scores(c, j):
        b0 = diag_block(c, j)
        b1 = pl.multiple_of(b0 + half, half)
        maxes = []
        for mp in range(2):
            k_lo = k_ref[mp, pl.ds(b0, half), :]
            k_hi = k_ref[mp, pl.ds(b1, half), :]
            q_lo = qt_ref[mp, :, pl.ds(b0, half)]
            q_hi = qt_ref[mp, :, pl.ds(b1, half)]
            s_ll = jnp.where(causal, jnp.dot(k_lo, q_lo, preferred_element_type=F32), -jnp.inf)
            s_lh = jnp.dot(k_lo, q_hi, preferred_element_type=F32)
            s_hh = jnp.where(causal, jnp.dot(k_hi, q_hi, preferred_element_type=F32), -jnp.inf)
            s_ref[c, mp, 0:half, 0:half] = s_ll
            s_ref[c, mp, 0:half, half:tk] = s_lh
            s_ref[c, mp, half:tk, half:tk] = s_hh
            maxes.append((jnp.max(s_ll, axis=0, keepdims=True),
                          jnp.maximum(jnp.max(s_lh, axis=0, keepdims=True),
                                      jnp.max(s_hh, axis=0, keepdims=True))))
        return tuple(maxes)

    def diag_numerators(c, j, maxes):
        for mp in range(2):
            m_lo, m_hi = maxes[mp]
            m_ref[c, j, mp, :, 0:half] = m_lo
            m_ref[c, j, mp, :, half:tk] = m_hi
            p_ref[c, mp, 0:half, 0:half] = jnp.exp2(s_ref[c, mp, 0:half, 0:half] - m_lo).astype(BF16)
            p_ref[c, mp, :, half:tk] = jnp.exp2(s_ref[c, mp, :, half:tk] - m_hi).astype(BF16)

    def diag_accumulate(c, j):
        vt = vt_ref[:, pl.ds(diag_block(c, j), tk)]
        for mp in range(2):
            acc_ref[c, j, mp, :, 0:half] = jnp.dot(
                vt[:, 0:half], p_ref[c, mp, 0:half, 0:half], preferred_element_type=F32)
            acc_ref[c, j, mp, :, half:tk] = jnp.dot(
                vt, p_ref[c, mp, :, half:tk], preferred_element_type=F32)

    def diag_body(j, carry):
        out = []
        for c in range(n_chain):
            diag_accumulate(c, j - 1)
            diag_numerators(c, j, carry[c])
            out.append(diag_scores(c, j + 1))
        return tuple(out)

    init = []
    for c in range(n_chain):
        diag_numerators(c, 0, diag_scores(c, 0))
        init.append(diag_scores(c, 1))
    lax.fori_loop(1, n_slots, diag_body, tuple(init))
    for c in range(n_chain):
        diag_accumulate(c, n_slots - 1)

    def entry(c, field, g):
        return full_ref[c, field, g + 1]

    def scores(c, g):
        k0 = pl.multiple_of(entry(c, SCHED_KB, g) * tk, tk)
        q0 = pl.multiple_of(entry(c, SCHED_QB, g) * tk, tk)
        maxes = []
        for mp in range(2):
            s = jnp.dot(k_ref[mp, pl.ds(k0, tk), :], qt_ref[mp, :, pl.ds(q0, tk)],
                        preferred_element_type=F32)
            s_ref[c, mp] = s
            maxes.append(jnp.max(s, axis=0, keepdims=True))
        return tuple(maxes)

    def numerators(c, g, ms, maxes):
        off = slope2 * ((entry(c, SCHED_KB, g) - entry(c, SCHED_QB, g)) * tk).astype(F32)
        resume = entry(c, SCHED_FIRST, g) == 1
        slot = entry(c, SCHED_SLOT, g)
        new_ms, scales = [], []
        for mp in range(2):
            m_old = jnp.where(resume, m_ref[c, slot, mp], ms[mp])
            m_new = jnp.maximum(m_old, maxes[mp] + off)
            scales.append(jnp.exp2(m_old - m_new))
            p_ref[c, mp] = jnp.exp2(s_ref[c, mp] - (m_new - off)).astype(BF16)
            new_ms.append(m_new)
        return tuple(new_ms), tuple(scales)

    def accumulate(c, g, scales):
        k0 = pl.multiple_of(entry(c, SCHED_KB, g) * tk, tk)
        slot = entry(c, SCHED_SLOT, g)
        vt = vt_ref[:, pl.ds(k0, tk)]
        for mp in range(2):
            acc_ref[c, slot, mp] = scales[mp] * acc_ref[c, slot, mp] + jnp.dot(
                vt, p_ref[c, mp], preferred_element_type=F32)

    def body(g, carry):
        out = []
        for c in range(n_chain):
            ms, scales, maxes = carry[c]
            accumulate(c, g - 1, scales)
            ms, scales = numerators(c, g, ms, maxes)
            maxes = scores(c, g + 1)
            out.append((ms, scales, maxes))
        return tuple(out)

    zero = jnp.zeros((1, tk), F32)
    init = []
    for c in range(n_chain):
        ms, scales = numerators(c, 0, (zero, zero), scores(c, 0))
        init.append((ms, scales, scores(c, 1)))
    state = lax.fori_loop(1, n_full, body, tuple(init))
    for c in range(n_chain):
        accumulate(c, n_full - 1, state[c][1])

    def finalize_slot(slot, carry):
        for c in range(n_chain):
            finalize(c, slot, diag_ref[c, slot + 1])
        return carry

    lax.fori_loop(0, n_slots, finalize_slot, 0)


def _attention(qt, k, vt, slopes, lamp, subg, lam_init):
    bsz, _, _, seq = qt.shape
    n = bsz * seq
    tk = ATTN_K
    vrows = V_DIM + ONES_ROWS
    blocks, diag, full = _chain_schedule(seq // tk, N_CHAIN)
    n_slots = len(blocks[0])
    n_full = full.shape[2] - 2
    return pl.pallas_call(
        functools.partial(_attn_kernel, lam_init, blocks, n_full),
        grid=(bsz, N_HEADS),
        in_specs=[
            pl.BlockSpec(memory_space=pltpu.SMEM),
            pl.BlockSpec(memory_space=pltpu.SMEM),
            pl.BlockSpec(memory_space=pltpu.SMEM),
            pl.BlockSpec((None, 2, SLAB, seq), lambda b, h: (b, 0, h, 0)),
            pl.BlockSpec((2, seq, SLAB), lambda b, h: (0, b, h)),
            pl.BlockSpec((None, vrows, seq), lambda b, h: (b, h, 0)),
            pl.BlockSpec((4, HEAD_DIM), lambda b, h: (0, 0)),
            pl.BlockSpec((V_DIM, 1), lambda b, h: (0, 0)),
        ],
        out_specs=pl.BlockSpec((seq, V_DIM), lambda b, h: (b, h)),
        out_shape=jax.ShapeDtypeStruct((n, D_ATTN), BF16),
        scratch_shapes=[
            pltpu.VMEM((N_CHAIN, n_slots, 2, vrows, tk), F32),
            pltpu.VMEM((N_CHAIN, n_slots, 2, 1, tk), F32),
            pltpu.VMEM((N_CHAIN, 2, tk, tk), F32),
            pltpu.VMEM((N_CHAIN, 2, tk, tk), BF16),
        ],
        compiler_params=pltpu.CompilerParams(
            dimension_semantics=("arbitrary", "arbitrary"),
            vmem_limit_bytes=VMEM_LIMIT_BYTES),
        name="diff_attn",
    )(jnp.asarray(diag), jnp.asarray(full), slopes, qt, k, vt, lamp, subg)


def _merge_ffn_kernel(x_ref, o_ref, gp_ref, gb_ref, wpa_ref, wo_ref, g2_ref, b2_ref,
                      wg_ref, wu_ref, wd_ref, g3_ref, b3_ref, out_ref, h_ref):
    y_attn = jnp.dot(o_ref[...], wpa_ref[...], preferred_element_type=F32)
    mixed = (gp_ref[...] + gb_ref[...] * y_attn).astype(BF16)
    mix = jnp.dot(mixed, wo_ref[...], preferred_element_type=F32)
    x2 = _layer_norm(DEEPNORM_ALPHA * x_ref[...] + mix, g2_ref[...], b2_ref[...])
    out_ref[...] = _ffn_ln_block(x2, wg_ref, wu_ref, wd_ref, g3_ref, b3_ref, h_ref)


def _merge_ffn(x2d, o, gp, gb, wpa, wo, g2, b2, wg, wu, wd, g3, b3):
    n = x2d.shape[0]
    tm = MERGE_ROWS
    row_spec = pl.BlockSpec((tm, D_MODEL), lambda i: (i, 0))
    vec_spec = _resident((1, D_MODEL))
    return pl.pallas_call(
        _merge_ffn_kernel,
        grid=(n // tm,),
        in_specs=[row_spec, row_spec, row_spec, row_spec,
                  _resident((D_ATTN, D_MODEL)), _resident((D_MODEL, D_MODEL)), vec_spec, vec_spec,
                  _resident((D_MODEL, D_FF)), _resident((D_MODEL, D_FF)), _resident((D_FF, D_MODEL)),
                  vec_spec, vec_spec],
        out_specs=row_spec,
        out_shape=jax.ShapeDtypeStruct((n, D_MODEL), F32),
        scratch_shapes=[pltpu.VMEM((tm, D_FF), BF16)],
        compiler_params=pltpu.CompilerParams(
            dimension_semantics=("arbitrary",), vmem_limit_bytes=VMEM_LIMIT_BYTES),
        name="merge_ffn",
    )(x2d, o, gp, gb, wpa, wo, g2, b2, wg, wu, wd, g3, b3)


def _slope_pieces(slopes):
    c = slopes * LOG2E
    pieces = []
    for _ in range(N_SPLIT):
        piece = c.astype(BF16).astype(F32)
        pieces.append(piece)
        c = c - piece
    aux = jnp.zeros((N_HEADS, HEAD_DIM), F32)
    aux = aux.at[:, :2 * N_SPLIT].set(jnp.stack(pieces + pieces, axis=1))
    return aux.reshape(D_QK, 1)


def kernel(x, ffn1_w_gate, ffn1_w_up, ffn1_w_down, ln1_g, ln1_b, w_in, b_gate, pool_w, pool_b, pool_scale, lambda_q1, lambda_k1, lambda_q2, lambda_k2, subln_g, w_proj_pool, w_proj_attn, w_out, ln2_g, ln2_b, ffn2_w_gate, ffn2_w_up, ffn2_w_down, ln3_g, ln3_b):
    bsz, seq, d = x.shape
    n = bsz * seq
    h = x.reshape(n, d)
    slopes = jnp.exp2(-8.0 / N_HEADS * jnp.arange(1, N_HEADS + 1, dtype=F32))
    qaux = _slope_pieces(slopes)
    o_q, o_k, o_v, o_g = D_POOL, D_POOL + 2 * D_QK, D_POOL + 4 * D_QK, D_POOL + 4 * D_QK + D_ATTN
    for l in range(DEPTH):
        lam_init = 0.8 - 0.6 * math.exp(-0.3 * l)
        h = _ffn_ln(h, ffn1_w_gate[l].astype(BF16), ffn1_w_up[l].astype(BF16),
                    ffn1_w_down[l].astype(BF16), ln1_g[l][None], ln1_b[l][None])

        wi = w_in[l]
        qt, k, vt, gp, gb = _in_proj(
            h, seq,
            wi[:, :o_q].astype(BF16),
            wi[:, o_q:o_k].T.astype(BF16),
            wi[:, o_k:o_v].astype(BF16),
            wi[:, o_v:o_g].T.astype(BF16),
            wi[:, o_g:].astype(BF16),
            b_gate[l][None],
            qaux,
            pool_w[l].astype(BF16),
            pool_b[l][:, None, :],
            pool_scale[l][None],
            w_proj_pool[l].astype(BF16))

        lamp = jnp.stack([lambda_q1[l], lambda_k1[l], lambda_q2[l], lambda_k2[l]]).astype(F32)
        o = _attention(qt, k, vt, slopes, lamp, subln_g[l][:, None].astype(F32), lam_init)

        h = _merge_ffn(h, o, gp, gb, w_proj_attn[l].astype(BF16), w_out[l].astype(BF16),
                       ln2_g[l][None], ln2_b[l][None],
                       ffn2_w_gate[l].astype(BF16), ffn2_w_up[l].astype(BF16),
                       ffn2_w_down[l].astype(BF16), ln3_g[l][None], ln3_b[l][None])
    return h.reshape(bsz, seq, d)
```

```python
import functools
import math

import jax
import jax.numpy as jnp
import numpy as np
from jax import lax
from jax.experimental import pallas as pl
from jax.experimental.pallas import tpu as pltpu

F32 = jnp.float32
BF16 = jnp.bfloat16

D_MODEL = 1024
D_FF = 2816
POOL_WINDOWS = (2, 4, 8, 16)
POOL_GROUP_DIM = 128
D_POOL = len(POOL_WINDOWS) * POOL_GROUP_DIM
N_HEADS = 8
HEAD_DIM = 64
V_DIM = 2 * HEAD_DIM
D_ATTN = N_HEADS * V_DIM
D_QK = N_HEADS * HEAD_DIM
LN_EPS = 1e-5
RMS_EPS = 1e-5
DEPTH = 1
DEEPNORM_ALPHA = (2.0 * DEPTH) ** 0.25
LOG2E = math.log2(math.e)
SUBLANES = 8
POOL_HISTORY = SUBLANES * (max(POOL_WINDOWS).bit_length() - 1)
ONES_ROWS = 16
SLAB = 2 * HEAD_DIM
BF16_EXACT_INT = 256
N_SPLIT = 3

VMEM_LIMIT_BYTES = 56 * 1024 * 1024

FFN_ROWS = 512
MERGE_ROWS = 512
FFN_CHUNK = 256
ATTN_K = 512
N_CHAIN = 2


def _resident(shape):
    nd = len(shape)
    return pl.BlockSpec(shape, lambda *_: (0,) * nd, pipeline_mode=pl.Buffered(1))


def _sigmoid(x):
    return 0.5 * jnp.tanh(0.5 * x) + 0.5


def _layer_norm(r, g, b):
    mu = jnp.mean(r, axis=-1, keepdims=True)
    c = r - mu
    var = jnp.mean(c * c, axis=-1, keepdims=True)
    return c * lax.rsqrt(var + LN_EPS) * g + b


def _ffn_ln_block(x, wg_ref, wu_ref, wd_ref, g_ref, b_ref, h_ref):
    xb = x.astype(BF16)
    for c in range(0, D_FF, FFN_CHUNK):
        gate = jnp.dot(xb, wg_ref[:, c:c + FFN_CHUNK], preferred_element_type=F32)
        up = jnp.dot(xb, wu_ref[:, c:c + FFN_CHUNK], preferred_element_type=F32)
        h_ref[:, c:c + FFN_CHUNK] = (gate * _sigmoid(gate) * up).astype(BF16)
    y = jnp.dot(h_ref[...], wd_ref[...], preferred_element_type=F32)
    return _layer_norm(DEEPNORM_ALPHA * x + 0.5 * y, g_ref[...], b_ref[...])


def _ffn_ln_kernel(x_ref, wg_ref, wu_ref, wd_ref, g_ref, b_ref, o_ref, h_ref):
    o_ref[...] = _ffn_ln_block(x_ref[...], wg_ref, wu_ref, wd_ref, g_ref, b_ref, h_ref)


def _ffn_ln(x2d, wg, wu, wd, g, b):
    n = x2d.shape[0]
    tm = FFN_ROWS
    return pl.pallas_call(
        _ffn_ln_kernel,
        grid=(n // tm,),
        in_specs=[
            pl.BlockSpec((tm, D_MODEL), lambda i: (i, 0)),
            _resident((D_MODEL, D_FF)),
            _resident((D_MODEL, D_FF)),
            _resident((D_FF, D_MODEL)),
            _resident((1, D_MODEL)),
            _resident((1, D_MODEL)),
        ],
        out_specs=pl.BlockSpec((tm, D_MODEL), lambda i: (i, 0)),
        out_shape=jax.ShapeDtypeStruct((n, D_MODEL), F32),
        scratch_shapes=[pltpu.VMEM((tm, D_FF), BF16)],
        compiler_params=pltpu.CompilerParams(
            dimension_semantics=("arbitrary",), vmem_limit_bytes=VMEM_LIMIT_BYTES),
        name="ffn_ln",
    )(x2d, wg, wu, wd, g, b)


def _trailing_window_sum(ubuf_ref, wbuf_ref, lanes, w, tm):
    end = POOL_HISTORY + tm
    read = lambda a, b: ubuf_ref[a:b, lanes]
    for k in range(1, w.bit_length()):
        span, start = w >> k, k * SUBLANES
        level = read(start, end) + read(start - span, end - span)
        if span == 1:
            return level[POOL_HISTORY - start:]
        dst = wbuf_ref.at[k % 2]
        dst[start:end, :] = level
        read = lambda a, b, dst=dst: dst[a:b, :]


def _in_proj_kernel(tiles_per_seq, x_ref, wpool_ref, wqt_ref, wk_ref, wvt_ref, wg_ref, bg_ref,
                    qaux_ref, poolw_ref, poolb_ref, pscale_ref, wpp_ref,
                    qt_ref, k_ref, vt_ref, gp_ref, gb_ref, ubuf_ref, wbuf_ref):
    i = pl.program_id(0)
    tm = x_ref.shape[0]
    xb = x_ref[...].astype(BF16)
    seq_tile = i % tiles_per_seq
    nt = (((1,), (1,)), ((), ()))

    qt = lax.dot_general(wqt_ref[...], xb, nt, preferred_element_type=F32)
    qt = (qt * (HEAD_DIM ** -0.5 * LOG2E)).astype(BF16)
    for hd in range(N_HEADS):
        aux = jnp.broadcast_to(qaux_ref[hd * HEAD_DIM:(hd + 1) * HEAD_DIM, :],
                               (HEAD_DIM, tm)).astype(BF16)
        feat_rows = hd * SLAB + (hd % 2) * HEAD_DIM
        aux_rows = hd * SLAB + (1 - hd % 2) * HEAD_DIM
        for mp in range(2):
            src = mp * D_QK + hd * HEAD_DIM
            qt_ref[mp, feat_rows:feat_rows + HEAD_DIM, :] = qt[src:src + HEAD_DIM]
            qt_ref[mp, aux_rows:aux_rows + HEAD_DIM, :] = aux

    k = jnp.dot(xb, wk_ref[...], preferred_element_type=F32)
    row = lax.broadcasted_iota(jnp.int32, (tm, SLAB), 0)
    lane = lax.broadcasted_iota(jnp.int32, (tm, SLAB), 1)
    j = (seq_tile * tm + row) & (ATTN_K - 1)
    jlo = (j & (BF16_EXACT_INT - 1)).astype(F32)
    jhi = (j - (j & (BF16_EXACT_INT - 1))).astype(F32)
    low = lane < HEAD_DIM
    al = lane & (HEAD_DIM - 1)
    aux = jnp.where(al < N_SPLIT, jlo, jnp.where(al < 2 * N_SPLIT, jhi, 0.0))
    for mp in range(2):
        for pair in range(N_HEADS // 2):
            src = mp * D_QK + pair * SLAB
            feats = k[:, src:src + SLAB]
            dst = 2 * pair * SLAB
            k_ref[mp, :, dst:dst + SLAB] = jnp.where(low, feats, aux).astype(BF16)
            k_ref[mp, :, dst + SLAB:dst + 2 * SLAB] = jnp.where(low, aux, feats).astype(BF16)

    vt = lax.dot_general(wvt_ref[...], xb, nt, preferred_element_type=F32)
    vrows = V_DIM + ONES_ROWS
    for hd in range(N_HEADS):
        vt_ref[hd * vrows:hd * vrows + V_DIM, :] = vt[hd * V_DIM:(hd + 1) * V_DIM].astype(BF16)
        vt_ref[hd * vrows + V_DIM:(hd + 1) * vrows, :] = jnp.ones((ONES_ROWS, tm), BF16)

    gates = _sigmoid(jnp.dot(xb, wg_ref[...], preferred_element_type=F32) + bg_ref[...])
    gb_ref[...] = gates[:, D_MODEL:]

    hist = POOL_HISTORY
    u = jnp.dot(xb, wpool_ref[...], preferred_element_type=F32)

    @pl.when(seq_tile == 0)
    def _():
        ubuf_ref[0:hist, :] = jnp.zeros((hist, D_POOL), F32)

    @pl.when(seq_tile != 0)
    def _():
        ubuf_ref[0:hist, :] = ubuf_ref[tm:tm + hist, :]

    ubuf_ref[hist:hist + tm, :] = u

    pos = seq_tile * tm + lax.broadcasted_iota(jnp.int32, (tm, 1), 0)
    zs = []
    for g, w in enumerate(POOL_WINDOWS):
        lanes = slice(g * POOL_GROUP_DIM, (g + 1) * POOL_GROUP_DIM)
        ug = u[:, lanes]
        win = _trailing_window_sum(ubuf_ref, wbuf_ref, lanes, w, tm)
        cnt = jnp.minimum(pos + 1, w).astype(F32)
        z = (win / cnt - ug).astype(BF16)
        zg = jnp.dot(z, poolw_ref[g], preferred_element_type=F32) + poolb_ref[g]
        zs.append(zg)
    zall = (jnp.concatenate(zs, axis=-1) * pscale_ref[...]).astype(BF16)
    y_pool = jnp.dot(zall, wpp_ref[...], preferred_element_type=F32)
    gp_ref[...] = gates[:, :D_MODEL] * y_pool


def _in_proj(x2d, seq, wpool, wqt, wk, wvt, wg, bg, qaux, poolw, poolb, pscale, wpp):
    n = x2d.shape[0]
    bsz = n // seq
    tm = FFN_ROWS
    tps = seq // tm
    row_spec = pl.BlockSpec((tm, D_MODEL), lambda i: (i, 0))
    vt_rows = N_HEADS * (V_DIM + ONES_ROWS)
    return pl.pallas_call(
        functools.partial(_in_proj_kernel, tps),
        grid=(n // tm,),
        in_specs=[
            row_spec,
            _resident((D_MODEL, D_POOL)),
            _resident((2 * D_QK, D_MODEL)),
            _resident((D_MODEL, 2 * D_QK)),
            _resident((D_ATTN, D_MODEL)),
            _resident((D_MODEL, 2 * D_MODEL)),
            _resident((1, 2 * D_MODEL)),
            _resident((D_QK, 1)),
            _resident((len(POOL_WINDOWS), POOL_GROUP_DIM, POOL_GROUP_DIM)),
            _resident((len(POOL_WINDOWS), 1, POOL_GROUP_DIM)),
            _resident((1, D_POOL)),
            _resident((D_POOL, D_MODEL)),
        ],
        out_specs=[
            pl.BlockSpec((None, 2, N_HEADS * SLAB, tm), lambda i: (i // tps, 0, 0, i % tps)),
            pl.BlockSpec((2, tm, N_HEADS * SLAB), lambda i: (0, i, 0)),
            pl.BlockSpec((None, vt_rows, tm), lambda i: (i // tps, 0, i % tps)),
            row_spec,
            row_spec,
        ],
        out_shape=[
            jax.ShapeDtypeStruct((bsz, 2, N_HEADS * SLAB, seq), BF16),
            jax.ShapeDtypeStruct((2, n, N_HEADS * SLAB), BF16),
            jax.ShapeDtypeStruct((bsz, vt_rows, seq), BF16),
            jax.ShapeDtypeStruct((n, D_MODEL), F32),
            jax.ShapeDtypeStruct((n, D_MODEL), F32),
        ],
        scratch_shapes=[pltpu.VMEM((POOL_HISTORY + tm, D_POOL), F32),
                        pltpu.VMEM((2, POOL_HISTORY + tm, POOL_GROUP_DIM), F32)],
        compiler_params=pltpu.CompilerParams(
            dimension_semantics=("arbitrary",), vmem_limit_bytes=VMEM_LIMIT_BYTES),
        name="in_proj",
    )(x2d, wpool, wqt, wk, wvt, wg, bg, qaux, poolw, poolb, pscale, wpp)


SCHED_KB, SCHED_QB, SCHED_SLOT = range(3)
N_SCHED_FIELDS = 3


def _chain_schedule(nq, n_chain):
    loads = [0] * n_chain
    blocks = [[] for _ in range(n_chain)]
    for qb in reversed(range(nq)):
        c = loads.index(min(loads))
        blocks[c].append(qb)
        loads[c] += qb + 1
    assert len({len(b) for b in blocks}) == 1, f"unequal slot counts over chains: {blocks}"
    diag = np.zeros((n_chain, len(blocks[0]) + 2), np.int32)
    for c in range(n_chain):
        diag[c] = [blocks[c][0]] + blocks[c] + [blocks[c][-1]]

    def table(steps_of):
        rows = []
        for c in range(n_chain):
            steps = [(kb, qb, slot) for slot, qb in enumerate(blocks[c]) for kb in steps_of(qb)]
            rows.append([steps[0]] + steps + [steps[-1]])
        assert len({len(r) for r in rows}) == 1, f"unequal step counts over chains: {blocks}"
        return np.array(rows, np.int32).transpose(0, 2, 1)

    pairs = table(lambda qb: range(0, qb - qb % 2, 2))
    singles = table(lambda qb: [qb - 1] if qb % 2 else [])
    return blocks, diag, pairs, singles


def _attn_kernel(lam_init, blocks, diag_ref, pairs_ref, singles_ref, slopes_ref, qt_ref, k_ref, vt_ref,
                 lamp_ref, subg_ref, o_ref, acc_ref, m_ref, s_ref, p_ref):
    h = pl.program_id(1)
    tk = ATTN_K
    n_chain = len(blocks)
    n_slots = len(blocks[0])
    slope2 = slopes_ref[h] * LOG2E

    lp = lamp_ref[...]
    lam = (jnp.exp(jnp.sum(lp[0:1] * lp[1:2], axis=-1, keepdims=True))
           - jnp.exp(jnp.sum(lp[2:3] * lp[3:4], axis=-1, keepdims=True)) + lam_init)

    def finalize(c, slot, qb):
        q0 = pl.multiple_of(qb * tk, tk)
        acc1 = acc_ref[c, slot, 0]
        acc2 = acc_ref[c, slot, 1]
        o = acc1[:V_DIM] / acc1[V_DIM:V_DIM + 1] - lam * (acc2[:V_DIM] / acc2[V_DIM:V_DIM + 1])
        o = o * lax.rsqrt(jnp.mean(o * o, axis=0, keepdims=True) + RMS_EPS)
        o = o * subg_ref[...] * (1.0 - lam_init)
        o_ref[pl.ds(q0, tk), :] = o.T.astype(o_ref.dtype)


    half = tk // 2
    causal = (lax.broadcasted_iota(jnp.int32, (half, half), 0)
              <= lax.broadcasted_iota(jnp.int32, (half, half), 1))

    def diag_block(c, j):
        return pl.multiple_of(diag_ref[c, j + 1] * tk, tk)

    def diag_scores(c, j):
        b0 = diag_block(c, j)
        b1 = pl.multiple_of(b0 + half, half)
        maxes = []
        for mp in range(2):
            k_lo = k_ref[mp, pl.ds(b0, half), :]
            k_hi = k_ref[mp, pl.ds(b1, half), :]
            q_lo = qt_ref[mp, :, pl.ds(b0, half)]
            q_hi = qt_ref[mp, :, pl.ds(b1, half)]
            s_ll = jnp.where(causal, jnp.dot(k_lo, q_lo, preferred_element_type=F32), -jnp.inf)
            s_lh = jnp.dot(k_lo, q_hi, preferred_element_type=F32)
            s_hh = jnp.where(causal, jnp.dot(k_hi, q_hi, preferred_element_type=F32), -jnp.inf)
            s_ref[c, mp, 0:half, 0:half] = s_ll
            s_ref[c, mp, 0:half, half:tk] = s_lh
            s_ref[c, mp, half:tk, half:tk] = s_hh
            maxes.append((jnp.max(s_ll, axis=0, keepdims=True),
                          jnp.maximum(jnp.max(s_lh, axis=0, keepdims=True),
                                      jnp.max(s_hh, axis=0, keepdims=True))))
        return tuple(maxes)

    def diag_numerators(c, j, maxes):
        for mp in range(2):
            m_lo, m_hi = maxes[mp]
            m_ref[c, j, mp, :, 0:half] = m_lo
            m_ref[c, j, mp, :, half:tk] = m_hi
            p_ref[c, mp, 0:half, 0:half] = jnp.exp2(s_ref[c, mp, 0:half, 0:half] - m_lo).astype(BF16)
            p_ref[c, mp, 0:tk, half:tk] = jnp.exp2(s_ref[c, mp, 0:tk, half:tk] - m_hi).astype(BF16)

    def diag_accumulate(c, j):
        vt = vt_ref[:, pl.ds(diag_block(c, j), tk)]
        for mp in range(2):
            acc_ref[c, j, mp, :, 0:half] = jnp.dot(
                vt[:, 0:half], p_ref[c, mp, 0:half, 0:half], preferred_element_type=F32)
            acc_ref[c, j, mp, :, half:tk] = jnp.dot(
                vt, p_ref[c, mp, 0:tk, half:tk], preferred_element_type=F32)

    def diag_body(j, carry):
        out = []
        for c in range(n_chain):
            diag_accumulate(c, j - 1)
            diag_numerators(c, j, carry[c])
            out.append(diag_scores(c, j + 1))
        return tuple(out)

    init = []
    for c in range(n_chain):
        diag_numerators(c, 0, diag_scores(c, 0))
        init.append(diag_scores(c, 1))
    lax.fori_loop(1, n_slots, diag_body, tuple(init))
    for c in range(n_chain):
        diag_accumulate(c, n_slots - 1)
    def run_pass(table_ref, nblk):
        n_steps = table_ref.shape[2] - 2
        keys = nblk * tk

        def entry(c, field, g):
            return table_ref[c, field, g + 1]

        def scores(c, g):
            k0 = entry(c, SCHED_KB, g) * tk
            q0 = pl.multiple_of(entry(c, SCHED_QB, g) * tk, tk)
            maxes = []
            for mp in range(2):
                q_slab = qt_ref[mp, :, pl.ds(q0, tk)]
                block_max = []
                for kb in range(nblk):
                    rows = pl.ds(pl.multiple_of(k0 + kb * tk, tk), tk)
                    s = jnp.dot(k_ref[mp, rows, :], q_slab, preferred_element_type=F32)
                    s_ref[c, mp, kb * tk:(kb + 1) * tk, :] = s
                    block_max.append(jnp.max(s, axis=0, keepdims=True))
                maxes.append(tuple(block_max))
            return tuple(maxes)

        def numerators(c, g, maxes):
            first = entry(c, SCHED_KB, g) - entry(c, SCHED_QB, g)
            offs = [slope2 * ((first + kb) * tk).astype(F32) for kb in range(nblk)]
            slot = entry(c, SCHED_SLOT, g)
            scales = []
            for mp in range(2):
                m_old = m_ref[c, slot, mp]
                m_new = m_old
                for kb in range(nblk):
                    m_new = jnp.maximum(m_new, maxes[mp][kb] + offs[kb])
                m_ref[c, slot, mp] = m_new
                scales.append(jnp.exp2(m_old - m_new))
                for kb in range(nblk):
                    rows = slice(kb * tk, (kb + 1) * tk)
                    p_ref[c, mp, rows, :] = jnp.exp2(
                        s_ref[c, mp, rows, :] - (m_new - offs[kb])).astype(BF16)
            return tuple(scales)

        def accumulate(c, g, scales, p_vals=None):
            k0 = pl.multiple_of(entry(c, SCHED_KB, g) * tk, tk)
            slot = entry(c, SCHED_SLOT, g)
            vt = vt_ref[:, pl.ds(k0, keys)]
            for mp in range(2):
                p = p_ref[c, mp, 0:keys, :] if p_vals is None else p_vals[mp]
                acc_ref[c, slot, mp] = scales[mp] * acc_ref[c, slot, mp] + jnp.dot(
                    vt, p, preferred_element_type=F32)

        def body(g, carry):
            out = []
            for c in range(n_chain):
                scales, maxes = carry[c]
                p_prev = tuple(p_ref[c, mp, 0:keys, :] for mp in range(2))
                new_scales = numerators(c, g, maxes)
                accumulate(c, g - 1, scales, p_prev)
                scales = new_scales
                out.append((scales, scores(c, g + 1)))
            return tuple(out)

        if n_steps == 0:
            return
        init = []
        for c in range(n_chain):
            scales = numerators(c, 0, scores(c, 0))
            init.append((scales, scores(c, 1)))
        state = lax.fori_loop(1, n_steps, body, tuple(init))
        for c in range(n_chain):
            accumulate(c, n_steps - 1, state[c][0])

    run_pass(pairs_ref, 2)
    run_pass(singles_ref, 1)

    def finalize_slot(slot, carry):
        for c in range(n_chain):
            finalize(c, slot, diag_ref[c, slot + 1])
        return carry

    lax.fori_loop(0, n_slots, finalize_slot, 0)


def _attention(qt, k, vt, slopes, lamp, subg, lam_init):
    bsz, _, _, seq = qt.shape
    n = bsz * seq
    tk = ATTN_K
    vrows = V_DIM + ONES_ROWS
    blocks, diag, pairs, singles = _chain_schedule(seq // tk, N_CHAIN)
    n_slots = len(blocks[0])
    return pl.pallas_call(
        functools.partial(_attn_kernel, lam_init, blocks),
        grid=(bsz, N_HEADS),
        in_specs=[
            pl.BlockSpec(memory_space=pltpu.SMEM),
            pl.BlockSpec(memory_space=pltpu.SMEM),
            pl.BlockSpec(memory_space=pltpu.SMEM),
            pl.BlockSpec(memory_space=pltpu.SMEM),
            pl.BlockSpec((None, 2, SLAB, seq), lambda b, h: (b, 0, h, 0)),
            pl.BlockSpec((2, seq, SLAB), lambda b, h: (0, b, h)),
            pl.BlockSpec((None, vrows, seq), lambda b, h: (b, h, 0)),
            pl.BlockSpec((4, HEAD_DIM), lambda b, h: (0, 0)),
            pl.BlockSpec((V_DIM, 1), lambda b, h: (0, 0)),
        ],
        out_specs=pl.BlockSpec((seq, V_DIM), lambda b, h: (b, h)),
        out_shape=jax.ShapeDtypeStruct((n, D_ATTN), BF16),
        scratch_shapes=[
            pltpu.VMEM((N_CHAIN, n_slots, 2, vrows, tk), F32),
            pltpu.VMEM((N_CHAIN, n_slots, 2, 1, tk), F32),
            pltpu.VMEM((N_CHAIN, 2, 2 * tk, tk), F32),
            pltpu.VMEM((N_CHAIN, 2, 2 * tk, tk), BF16),
        ],
        compiler_params=pltpu.CompilerParams(
            dimension_semantics=("arbitrary", "arbitrary"),
            vmem_limit_bytes=VMEM_LIMIT_BYTES),
        name="diff_attn",
    )(jnp.asarray(diag), jnp.asarray(pairs), jnp.asarray(singles), slopes, qt, k, vt, lamp, subg)


def _merge_ffn_kernel(x_ref, o_ref, gp_ref, gb_ref, wpa_ref, wo_ref, g2_ref, b2_ref,
                      wg_ref, wu_ref, wd_ref, g3_ref, b3_ref, out_ref, h_ref):
    y_attn = jnp.dot(o_ref[...], wpa_ref[...], preferred_element_type=F32)
    mixed = (gp_ref[...] + gb_ref[...] * y_attn).astype(BF16)
    mix = jnp.dot(mixed, wo_ref[...], preferred_element_type=F32)
    x2 = _layer_norm(DEEPNORM_ALPHA * x_ref[...] + mix, g2_ref[...], b2_ref[...])
    out_ref[...] = _ffn_ln_block(x2, wg_ref, wu_ref, wd_ref, g3_ref, b3_ref, h_ref)


def _merge_ffn(x2d, o, gp, gb, wpa, wo, g2, b2, wg, wu, wd, g3, b3):
    n = x2d.shape[0]
    tm = MERGE_ROWS
    row_spec = pl.BlockSpec((tm, D_MODEL), lambda i: (i, 0))
    vec_spec = _resident((1, D_MODEL))
    return pl.pallas_call(
        _merge_ffn_kernel,
        grid=(n // tm,),
        in_specs=[row_spec, row_spec, row_spec, row_spec,
                  _resident((D_ATTN, D_MODEL)), _resident((D_MODEL, D_MODEL)), vec_spec, vec_spec,
                  _resident((D_MODEL, D_FF)), _resident((D_MODEL, D_FF)), _resident((D_FF, D_MODEL)),
                  vec_spec, vec_spec],
        out_specs=row_spec,
        out_shape=jax.ShapeDtypeStruct((n, D_MODEL), F32),
        scratch_shapes=[pltpu.VMEM((tm, D_FF), BF16)],
        compiler_params=pltpu.CompilerParams(
            dimension_semantics=("arbitrary",), vmem_limit_bytes=VMEM_LIMIT_BYTES),
        name="merge_ffn",
    )(x2d, o, gp, gb, wpa, wo, g2, b2, wg, wu, wd, g3, b3)


def _slope_pieces(slopes):
    c = slopes * LOG2E
    pieces = []
    for _ in range(N_SPLIT):
        piece = c.astype(BF16).astype(F32)
        pieces.append(piece)
        c = c - piece
    aux = jnp.zeros((N_HEADS, HEAD_DIM), F32)
    aux = aux.at[:, :2 * N_SPLIT].set(jnp.stack(pieces + pieces, axis=1))
    return aux.reshape(D_QK, 1)


def kernel(x, ffn1_w_gate, ffn1_w_up, ffn1_w_down, ln1_g, ln1_b, w_in, b_gate, pool_w, pool_b, pool_scale, lambda_q1, lambda_k1, lambda_q2, lambda_k2, subln_g, w_proj_pool, w_proj_attn, w_out, ln2_g, ln2_b, ffn2_w_gate, ffn2_w_up, ffn2_w_down, ln3_g, ln3_b):
    bsz, seq, d = x.shape
    n = bsz * seq
    h = x.reshape(n, d)
    slopes = jnp.exp2(-8.0 / N_HEADS * jnp.arange(1, N_HEADS + 1, dtype=F32))
    qaux = _slope_pieces(slopes)
    o_q, o_k, o_v, o_g = D_POOL, D_POOL + 2 * D_QK, D_POOL + 4 * D_QK, D_POOL + 4 * D_QK + D_ATTN
    for l in range(DEPTH):
        lam_init = 0.8 - 0.6 * math.exp(-0.3 * l)
        h = _ffn_ln(h, ffn1_w_gate[l].astype(BF16), ffn1_w_up[l].astype(BF16),
                    ffn1_w_down[l].astype(BF16), ln1_g[l][None], ln1_b[l][None])

        wi = w_in[l]
        qt, k, vt, gp, gb = _in_proj(
            h, seq,
            wi[:, :o_q].astype(BF16),
            wi[:, o_q:o_k].T.astype(BF16),
            wi[:, o_k:o_v].astype(BF16),
            wi[:, o_v:o_g].T.astype(BF16),
            wi[:, o_g:].astype(BF16),
            b_gate[l][None],
            qaux,
            pool_w[l].astype(BF16),
            pool_b[l][:, None, :],
            pool_scale[l][None],
            w_proj_pool[l].astype(BF16))

        lamp = jnp.stack([lambda_q1[l], lambda_k1[l], lambda_q2[l], lambda_k2[l]]).astype(F32)
        o = _attention(qt, k, vt, slopes, lamp, subln_g[l][:, None].astype(F32), lam_init)

        h = _merge_ffn(h, o, gp, gb, w_proj_attn[l].astype(BF16), w_out[l].astype(BF16),
                       ln2_g[l][None], ln2_b[l][None],
                       ffn2_w_gate[l].astype(BF16), ffn2_w_up[l].astype(BF16),
                       ffn2_w_down[l].astype(BF16), ln3_g[l][None], ln3_b[l][None])
    return h.reshape(bsz, seq, d)
```
